```python
import math, functools
import jax, jax.numpy as jnp
from jax import lax
import numpy as np

D_MODEL = 1024
BATCH = 8
SEQ = 2048
DEPTH = 2
DEC_BATCH = 128
DEC_SEQ = 8
PAST_LEN = 16384
PAGE_SIZE = 128

N_META = 16
CHUNK = 64
EPS = 1e-6
CONV_W = 4
N_BRANCH = 3
ML_HEADS = 4
ML_DQK = D_MODEL // 8
ML_DV = D_MODEL // 4
ML_QK = ML_HEADS * ML_DQK
ML_V = ML_HEADS * ML_DV
LRU_W = D_MODEL
LRU_BLOCKS = 16
LRU_BD = LRU_W // LRU_BLOCKS
LRU_C = 8.0
SSD_HEADS = 16
SSD_P = D_MODEL // SSD_HEADS
SSD_INNER = SSD_HEADS * SSD_P
SSD_GROUPS = 4
SSD_N = 128
SSD_CONV_CH = SSD_INNER + 2 * SSD_GROUPS * SSD_N
IN_SPLITS = (ML_QK, ML_QK, ML_V, ML_HEADS, ML_HEADS, ML_V, ML_V,
             LRU_W, LRU_W,
             SSD_INNER, SSD_CONV_CH, SSD_HEADS,
             N_BRANCH * D_MODEL)
N_IN = 2 * ML_QK + 3 * ML_V + 2 * ML_HEADS + 2 * LRU_W + SSD_INNER + SSD_CONV_CH + SSD_HEADS + N_BRANCH * D_MODEL

kernel_name = 'hybrid_mlstm_rglru_ssd_decode_step'


def rmsnorm(x, g):
    xf = x.astype(jnp.float32)
    return xf * lax.rsqrt(jnp.mean(xf * xf, axis=-1, keepdims=True) + EPS) * g.astype(jnp.float32)


def split_cols(u):
    offs = np.cumsum((0,) + IN_SPLITS)
    return [u[..., int(offs[j]):int(offs[j + 1])] for j in range(len(IN_SPLITS))]


def _chunk(t):
    return math.gcd(t, CHUNK)


def causal_dwconv(x, buf, w, b):
    t = x.shape[1]
    xp = jnp.concatenate([buf, x], axis=1)
    y = b + xp[:, 0:t] * w[0]
    for j in range(1, CONV_W):
        y = y + xp[:, j:j + t] * w[j]
    return y, xp[:, -(CONV_W - 1):]


def _segmented(scan_fn, n_lead, seqs, state):
    if n_lead == 0:
        return scan_fn(*seqs, *state)
    out0, *state = scan_fn(*(s[:, :n_lead] for s in seqs), *state)
    out1, *state = scan_fn(*(s[:, n_lead:] for s in seqs), *state)
    return (jnp.concatenate([out0, out1], axis=1), *state)


def mlstm_scan(q, k, v, logi, logf, c0, n0, m0):
    bsz, t, h, _ = q.shape
    L = _chunk(t)
    nc = t // L

    def to_chunks(a):
        return jnp.moveaxis(a.reshape((bsz, nc, L) + a.shape[2:]), 1, 0)

    xs = tuple(to_chunks(a) for a in (q, k, v, logi, logf))
    causal = jnp.tril(jnp.ones((L, L), bool))[None, :, :, None]

    def step(carry, inp):
        c, n, m = carry
        qc, kc, vc, li, lf = inp
        b = jnp.cumsum(lf, axis=1)
        d = jnp.where(causal, b[:, :, None, :] - b[:, None, :, :] + li[:, None, :, :], -jnp.inf)
        inter = b + m[:, None, :]
        mt = jnp.maximum(inter, jnp.max(d, axis=2))
        s = jnp.einsum('bthd,bshd->btsh', qc, kc) * jnp.exp(d - mt[:, :, None, :])
        w_inter = jnp.exp(inter - mt)
        num = jnp.einsum('btsh,bshv->bthv', s, vc) + w_inter[..., None] * jnp.einsum('bthd,bhdv->bthv', qc, c)
        den = jnp.sum(s, axis=2) + w_inter * jnp.einsum('bthd,bhd->bth', qc, n)
        hout = num / jnp.maximum(jnp.abs(den), jnp.exp(-mt))[..., None]
        bl = b[:, -1]
        g = bl[:, None, :] - b + li
        m_new = jnp.maximum(bl + m, jnp.max(g, axis=1))
        ws = jnp.exp(g - m_new[:, None, :])
        wc = jnp.exp(bl + m - m_new)
        c_new = wc[..., None, None] * c + jnp.einsum('bsh,bshd,bshv->bhdv', ws, kc, vc)
        n_new = wc[..., None] * n + jnp.einsum('bsh,bshd->bhd', ws, kc)
        return (c_new, n_new, m_new), hout

    (c, n, m), hs = lax.scan(step, (c0, n0, m0), xs)
    hout = jnp.moveaxis(hs, 0, 1).reshape(bsz, t, h, v.shape[-1])
    return hout, c, n, m


def ssd_scan(xh, dt, bm, cm, s0, a):
    bsz, t, h, p = xh.shape
    g = bm.shape[2]
    e = h // g
    L = _chunk(t)
    nc = t // L

    def to_chunks(v):
        return jnp.moveaxis(v.reshape((bsz, nc, L) + v.shape[2:]), 1, 0)

    xs = tuple(to_chunks(v) for v in (xh.reshape(bsz, t, g, e, p), dt.reshape(bsz, t, g, e), bm, cm))
    causal = jnp.tril(jnp.ones((L, L), bool))[None, :, :, None, None]
    ag = a.reshape(g, e)

    def step(s, inp):
        xc, dtc, bc, cc = inp
        cum = jnp.cumsum(dtc * ag, axis=1)
        seg = jnp.where(causal, cum[:, :, None] - cum[:, None, :], -jnp.inf)
        w = jnp.einsum('btgn,bsgn->btsg', cc, bc)[..., None] * jnp.exp(seg) * dtc[:, None]
        y = jnp.einsum('btsge,bsgep->btgep', w, xc)
        y = y + jnp.einsum('btgn,bgepn->btgep', cc, s) * jnp.exp(cum)[..., None]
        cl = cum[:, -1]
        ws = jnp.exp(cl[:, None] - cum) * dtc
        s_new = jnp.exp(cl)[..., None, None] * s + jnp.einsum('bsge,bsgep,bsgn->bgepn', ws, xc, bc)
        return s_new, y

    s, ys = lax.scan(step, s0.reshape(bsz, g, e, p, -1), xs)
    y = jnp.moveaxis(ys, 0, 1).reshape(bsz, t, h, p)
    return y, s.reshape(bsz, h, p, -1)


def rglru(x, h0, w_a, b_a, w_x, b_x, lam):
    bsz, t, w = x.shape
    xb = x.reshape(bsz, t, LRU_BLOCKS, LRU_BD)
    r = jax.nn.sigmoid(jnp.einsum('btnd,nde->btne', xb, w_a).reshape(bsz, t, w) + b_a)
    i = jax.nn.sigmoid(jnp.einsum('btnd,nde->btne', xb, w_x).reshape(bsz, t, w) + b_x)
    log_a = -LRU_C * r * jax.nn.softplus(-lam)
    a = jnp.exp(log_a)
    u = jnp.sqrt(-jnp.expm1(2.0 * log_a)) * (i * x)
    u = u.at[:, 0].add(a[:, 0] * h0)

    def comb(l, rr):
        a1, b1 = l
        a2, b2 = rr
        return a1 * a2, a2 * b1 + b2

    _, hs = lax.associative_scan(comb, (a, u), axis=1)
    return hs, hs[:, -1]


def mixer_layer(x, n_lead, st, w_in, norm_g, ml_f_bias, ml_norm_g, lru_conv_w, lru_conv_b,
                lru_w_a, lru_b_a, lru_w_x, lru_b_x, lru_lambda, ssd_conv_w, ssd_conv_b,
                ssd_dt_bias, ssd_a_log, ssd_d, ssd_norm_g, w_br_ml, w_br_lru, w_br_ssd, w_out):
    c0, n0, m0, hl0, cl0, hs0, cs0 = (s.astype(jnp.float32) for s in st)
    bsz, t, _ = x.shape
    u = jnp.einsum('btd,de->bte', rmsnorm(x, norm_g), w_in)
    (ml_q, ml_k, ml_v, ml_i, ml_f, ml_o, ml_z, lru_x, lru_z,
     ssd_z, ssd_xbc, ssd_dt, gates) = split_cols(u)

    q = ml_q.reshape(bsz, t, ML_HEADS, ML_DQK)
    k = ml_k.reshape(bsz, t, ML_HEADS, ML_DQK) * (ML_DQK ** -0.5)
    v = ml_v.reshape(bsz, t, ML_HEADS, ML_DV)
    logf = jax.nn.log_sigmoid(ml_f + ml_f_bias)
    h_ml, c1, n1, m1 = _segmented(mlstm_scan, n_lead, (q, k, v, ml_i, logf), (c0, n0, m0))
    y_ml = rmsnorm(h_ml, ml_norm_g).reshape(bsz, t, ML_V) * jax.nn.sigmoid(ml_o) * jax.nn.silu(ml_z)

    xc, cl1 = causal_dwconv(lru_x, cl0, lru_conv_w, lru_conv_b)
    h_lru, hl1 = rglru(xc, hl0, lru_w_a, lru_b_a, lru_w_x, lru_b_x, lru_lambda)
    y_lru = h_lru * jax.nn.silu(lru_z)

    xbc, cs1 = causal_dwconv(ssd_xbc, cs0, ssd_conv_w, ssd_conv_b)
    xbc = jax.nn.silu(xbc)
    xs = xbc[..., :SSD_INNER].reshape(bsz, t, SSD_HEADS, SSD_P)
    bm = xbc[..., SSD_INNER:SSD_INNER + SSD_GROUPS * SSD_N].reshape(bsz, t, SSD_GROUPS, SSD_N)
    cm = xbc[..., SSD_INNER + SSD_GROUPS * SSD_N:].reshape(bsz, t, SSD_GROUPS, SSD_N)
    dt = jax.nn.softplus(ssd_dt + ssd_dt_bias)
    a = -jnp.exp(ssd_a_log.astype(jnp.float32))
    y_s, hs1 = _segmented(functools.partial(ssd_scan, a=a), n_lead, (xs, dt, bm, cm), (hs0,))
    y_s = (y_s + ssd_d[:, None] * xs).reshape(bsz, t, SSD_INNER) * jax.nn.silu(ssd_z)
    y_ssd = rmsnorm(y_s, ssd_norm_g)

    gt = jax.nn.sigmoid(gates).reshape(bsz, t, N_BRANCH, D_MODEL)
    merged = (gt[:, :, 0] * jnp.einsum('btw,wd->btd', y_ml, w_br_ml)
              + gt[:, :, 1] * jnp.einsum('btw,wd->btd', y_lru, w_br_lru)
              + gt[:, :, 2] * jnp.einsum('btw,wd->btd', y_ssd, w_br_ssd))
    out = jnp.einsum('btd,de->bte', merged, w_out)
    x_new = (x.astype(jnp.float32) + out).astype(x.dtype)
    return x_new, (c1, n1, m1, hl1, cl1, hs1, cs1)


def run_trunk(x, n_lead, states, layer_params, final_norm_g):
    new = []
    for l in range(DEPTH):
        x, st = mixer_layer(x, n_lead, tuple(s[l] for s in states), *(p[l] for p in layer_params))
        new.append(st)
    y = rmsnorm(x, final_norm_g).astype(x.dtype)
    stacked = tuple(jnp.stack([new[l][j] for l in range(DEPTH)]) for j in range(len(states)))
    return y, stacked


def setup_inputs(seed: int = 0) -> dict:
    key = jax.random.key(seed)
    ks = iter(jax.random.split(key, 48))

    def nrm(shape, scale):
        return scale * jax.random.normal(next(ks), shape, jnp.float32)

    def uni(shape, lo, hi):
        return jax.random.uniform(next(ks), shape, jnp.float32, lo, hi)

    x_prompt = nrm((BATCH, SEQ, D_MODEL), 1.0)
    x_sample = nrm((DEC_BATCH, DEC_SEQ, D_MODEL), 1.0)
    state_mlstm_c = nrm((DEPTH, DEC_BATCH, ML_HEADS, ML_DQK, ML_DV), 0.1)
    state_mlstm_n = nrm((DEPTH, DEC_BATCH, ML_HEADS, ML_DQK), 0.1)
    state_mlstm_m = nrm((DEPTH, DEC_BATCH, ML_HEADS), 0.5)
    state_rglru_h = nrm((DEPTH, DEC_BATCH, LRU_W), 0.5)
    state_rglru_conv = nrm((DEPTH, DEC_BATCH, CONV_W - 1, LRU_W), 1.0)
    state_ssd_h = nrm((DEPTH, DEC_BATCH, SSD_HEADS, SSD_P, SSD_N), 0.1)
    state_ssd_conv = nrm((DEPTH, DEC_BATCH, CONV_W - 1, SSD_CONV_CH), 1.0)
    meta_tokens = nrm((N_META, D_MODEL), 1.0)
    w_in = nrm((DEPTH, D_MODEL, N_IN), D_MODEL ** -0.5)
    norm_g = 1.0 + nrm((DEPTH, D_MODEL), 0.02)
    ml_f_bias = jnp.linspace(3.0, 6.0, ML_HEADS, dtype=jnp.float32)[None] + nrm((DEPTH, ML_HEADS), 0.1)
    ml_norm_g = 1.0 + nrm((DEPTH, ML_HEADS, ML_DV), 0.02)
    lru_conv_w = nrm((DEPTH, CONV_W, LRU_W), CONV_W ** -0.5)
    lru_conv_b = nrm((DEPTH, LRU_W), 0.02)
    lru_w_a = nrm((DEPTH, LRU_BLOCKS, LRU_BD, LRU_BD), LRU_BD ** -0.5)
    lru_b_a = nrm((DEPTH, LRU_W), 0.02)
    lru_w_x = nrm((DEPTH, LRU_BLOCKS, LRU_BD, LRU_BD), LRU_BD ** -0.5)
    lru_b_x = nrm((DEPTH, LRU_W), 0.02)
    a_c = uni((DEPTH, LRU_W), 0.9, 0.999) ** (1.0 / LRU_C)
    lru_lambda = jnp.log(a_c) - jnp.log1p(-a_c)
    ssd_conv_w = nrm((DEPTH, CONV_W, SSD_CONV_CH), CONV_W ** -0.5)
    ssd_conv_b = nrm((DEPTH, SSD_CONV_CH), 0.02)
    dt0 = jnp.exp(uni((DEPTH, SSD_HEADS), math.log(1e-3), math.log(1e-1)))
    ssd_dt_bias = dt0 + jnp.log(-jnp.expm1(-dt0))
    ssd_a_log = jnp.log(uni((DEPTH, SSD_HEADS), 1.0, 16.0))
    ssd_d = 1.0 + nrm((DEPTH, SSD_HEADS), 0.1)
    ssd_norm_g = 1.0 + nrm((DEPTH, SSD_INNER), 0.02)
    w_br_ml = nrm((DEPTH, ML_V, D_MODEL), ML_V ** -0.5)
    w_br_lru = nrm((DEPTH, LRU_W, D_MODEL), LRU_W ** -0.5)
    w_br_ssd = nrm((DEPTH, SSD_INNER, D_MODEL), SSD_INNER ** -0.5)
    w_out = nrm((DEPTH, D_MODEL, D_MODEL), D_MODEL ** -0.5)
    final_norm_g = 1.0 + nrm((D_MODEL,), 0.02)
    return {'x_prompt': x_prompt, 'x_sample': x_sample,
            'state_mlstm_c': state_mlstm_c, 'state_mlstm_n': state_mlstm_n, 'state_mlstm_m': state_mlstm_m,
            'state_rglru_h': state_rglru_h, 'state_rglru_conv': state_rglru_conv,
            'state_ssd_h': state_ssd_h, 'state_ssd_conv': state_ssd_conv,
            'meta_tokens': meta_tokens, 'w_in': w_in, 'norm_g': norm_g, 'ml_f_bias': ml_f_bias,
            'ml_norm_g': ml_norm_g, 'lru_conv_w': lru_conv_w, 'lru_conv_b': lru_conv_b,
            'lru_w_a': lru_w_a, 'lru_b_a': lru_b_a, 'lru_w_x': lru_w_x, 'lru_b_x': lru_b_x,
            'lru_lambda': lru_lambda, 'ssd_conv_w': ssd_conv_w, 'ssd_conv_b': ssd_conv_b,
            'ssd_dt_bias': ssd_dt_bias, 'ssd_a_log': ssd_a_log, 'ssd_d': ssd_d, 'ssd_norm_g': ssd_norm_g,
            'w_br_ml': w_br_ml, 'w_br_lru': w_br_lru, 'w_br_ssd': w_br_ssd, 'w_out': w_out,
            'final_norm_g': final_norm_g}


def reference(x_prompt, x_sample, state_mlstm_c, state_mlstm_n, state_mlstm_m, state_rglru_h,
              state_rglru_conv, state_ssd_h, state_ssd_conv, meta_tokens, w_in, norm_g, ml_f_bias,
              ml_norm_g, lru_conv_w, lru_conv_b, lru_w_a, lru_b_a, lru_w_x, lru_b_x, lru_lambda,
              ssd_conv_w, ssd_conv_b, ssd_dt_bias, ssd_a_log, ssd_d, ssd_norm_g,
              w_br_ml, w_br_lru, w_br_ssd, w_out, final_norm_g):
    layer_params = (w_in, norm_g, ml_f_bias, ml_norm_g, lru_conv_w, lru_conv_b, lru_w_a, lru_b_a,
                    lru_w_x, lru_b_x, lru_lambda, ssd_conv_w, ssd_conv_b, ssd_dt_bias, ssd_a_log,
                    ssd_d, ssd_norm_g, w_br_ml, w_br_lru, w_br_ssd, w_out)
    sample_states = (state_mlstm_c, state_mlstm_n, state_mlstm_m, state_rglru_h, state_rglru_conv,
                     state_ssd_h, state_ssd_conv)
    bp = x_prompt.shape[0]
    meta = jnp.broadcast_to(meta_tokens.astype(x_prompt.dtype)[None], (bp, N_META, D_MODEL))
    xp = jnp.concatenate([meta, x_prompt], axis=1)
    zero_states = tuple(jnp.zeros((DEPTH, bp) + s.shape[2:], jnp.float32) for s in sample_states)
    yp, (p_c, p_n, p_m, p_hl, p_cl, p_hs, p_cs) = run_trunk(xp, N_META, zero_states, layer_params, final_norm_g)
    y_prompt = yp[:, N_META:]
    y_sample, (s_c, s_n, s_m, s_hl, s_cl, s_hs, s_cs) = run_trunk(x_sample, 0, sample_states, layer_params, final_norm_g)
    return (y_prompt, y_sample, p_c, p_n, p_m, p_hl, p_cl, p_hs, p_cs, s_c, s_n, s_m, s_hl, s_cl, s_hs, s_cs)
```

```python
import functools

import jax
import jax.numpy as jnp
from jax import lax
from jax.experimental import pallas as pl
from jax.experimental.pallas import tpu as pltpu

f32 = jnp.float32
bf16 = jnp.bfloat16

D_MODEL = 1024
DEPTH = 2
N_META = 16
EPS = 1e-6
CONV_W = 4
ML_HEADS = 4
ML_DQK = 128
ML_DV = 256
ML_SCALE = ML_DQK ** -0.5
LRU_W = 1024
LRU_BLOCKS = 16
LRU_BD = 64
LRU_C = 8.0
SSD_HEADS = 16
SSD_P = 64
SSD_GROUPS = 4
SSD_N = 128
SSD_CONV_CH = 2048

COLBLK = 1024
BLK_QK, BLK_V, BLK_O, BLK_Z, BLK_LX, BLK_LZ, BLK_SZ, BLK_SX, BLK_SBC = range(9)
N_MIX = 9 * COLBLK
N_GATE = 3 * D_MODEL
N_MAIN = N_MIX + N_GATE
N_SMALL = 128
LANE_I, LANE_F, LANE_DT = 0, 4, 8

NEG = -1e30
VMEM_LIMIT = 56 * 1024 * 1024


def _cparams(sem):
    return pltpu.CompilerParams(dimension_semantics=sem, vmem_limit_bytes=VMEM_LIMIT)


def _proj_kernel(x_ref, g_ref, w_ref, ws_ref, u_ref, us_ref, xn_ref):
    @pl.when(pl.program_id(1) == 0)
    def _():
        x = x_ref[...]
        xn = x * lax.rsqrt(jnp.mean(x * x, axis=-1, keepdims=True) + EPS) * g_ref[...]
        xn_ref[...] = xn.astype(bf16)
        us_ref[...] = jnp.dot(xn_ref[...], ws_ref[...], preferred_element_type=f32)

    u_ref[...] = jnp.dot(xn_ref[...], w_ref[...], preferred_element_type=f32)


def _proj(x2d, g, w_main, w_small):
    t = x2d.shape[0]
    tm = min(t, 1024)
    tn = 2048
    return pl.pallas_call(
        _proj_kernel,
        grid=(t // tm, N_MAIN // tn),
        in_specs=[
            pl.BlockSpec((tm, D_MODEL), lambda i, j: (i, 0)),
            pl.BlockSpec((1, D_MODEL), lambda i, j: (0, 0)),
            pl.BlockSpec((D_MODEL, tn), lambda i, j: (0, j)),
            pl.BlockSpec((D_MODEL, N_SMALL), lambda i, j: (0, 0)),
        ],
        out_specs=[
            pl.BlockSpec((tm, tn), lambda i, j: (i, j)),
            pl.BlockSpec((tm, N_SMALL), lambda i, j: (i, 0)),
        ],
        out_shape=[
            jax.ShapeDtypeStruct((t, N_MAIN), f32),
            jax.ShapeDtypeStruct((t, N_SMALL), f32),
        ],
        scratch_shapes=[pltpu.VMEM((tm, D_MODEL), bf16)],
        compiler_params=_cparams(("arbitrary", "arbitrary")),
        name="proj",
    )(x2d, g, w_main, w_small)


def _cumsum_rows(x3, seg_len):
    sb = x3.shape[0]
    x = x3.reshape(sb * seg_len, x3.shape[2])
    t = lax.broadcasted_iota(jnp.int32, x.shape, 0) & (seg_len - 1)
    s = 1
    while s < seg_len:
        x = x + jnp.where(t >= s, pltpu.roll(x, s, 0), 0.0)
        s *= 2
    return x.reshape(x3.shape)


def _neg_expm1(x):
    u = jnp.exp(x)
    return jnp.where(u == 1.0, -x, (1.0 - u) * x / jnp.log(u))


def _mixer_kernel(L, SB, NC, *refs):
    (qk_ref, v_ref, o_ref, z_ref, lx_ref, lz_ref, sz_ref, sx_ref, sbc_ref, sm_ref,
     c_in, n_in, m_in, hl_in, cl_in, hs_in, cs_in,
     fb_ref, mg_ref, lcw_ref, lcb_ref, wax_ref, ba_ref, bx_ref, lam_ref,
     scw_ref, scb_ref, dtb_ref, alog_ref, sd_ref, sg_ref,
     y_ref, c_out, n_out, m_out, hl_out, cl_out, hs_out, cs_out,
     lpad, spad, a_s, u_s, h_s, ys_s) = refs

    if NC == 1:
        c_src, n_src, m_src, hl_src, cl_src, hs_src, cs_src = c_in, n_in, m_in, hl_in, cl_in, hs_in, cs_in
    else:
        c_src, n_src, m_src, hl_src, cl_src, hs_src, cs_src = c_out, n_out, m_out, hl_out, cl_out, hs_out, cs_out

        @pl.when(pl.program_id(1) == 0)
        def _():
            for src, dst in ((c_in, c_out), (n_in, n_out), (m_in, m_out), (hl_in, hl_out),
                             (cl_in, cl_out), (hs_in, hs_out), (cs_in, cs_out)):
                dst[...] = jnp.broadcast_to(src[...], dst.shape)

    lane = lax.broadcasted_iota(jnp.int32, (1, 1, N_SMALL), 2)
    is_f = jnp.abs(2 * lane - (2 * LANE_F + ML_HEADS - 1)) < ML_HEADS
    is_dt = jnp.abs(2 * lane - (2 * LANE_DT + SSD_HEADS - 1)) < SSD_HEADS
    ri = lax.broadcasted_iota(jnp.int32, (1, L, L), 1)
    ci = lax.broadcasted_iota(jnp.int32, (1, L, L), 2)
    causal = ri >= ci

    sm = sm_ref[...]
    logf = jax.nn.log_sigmoid(sm + fb_ref[...][None])
    dt = jax.nn.softplus(sm + dtb_ref[...][None])
    a_neg = -jnp.exp(alog_ref[...])[None]
    cs = _cumsum_rows(jnp.where(is_f, logf, jnp.where(is_dt, dt * a_neg, 0.0)), L)
    li4 = pltpu.roll(sm.reshape(SB * L, N_SMALL), LANE_F - LANE_I, 1).reshape(SB, L, N_SMALL)

    m0 = m_src[...]
    bl = cs[:, L - 1:L, :]
    inter = cs + m0
    g = bl - cs + li4
    m_new = jnp.maximum(bl + m0, jnp.max(g, axis=1, keepdims=True))
    ws = jnp.exp(g - m_new) * ML_SCALE
    wc = jnp.exp(bl + m0 - m_new)
    rows = jnp.swapaxes(jnp.where(is_f, cs - li4, cs), 1, 2)
    dt_rows = jnp.swapaxes(dt, 1, 2)

    for h in range(ML_HEADS):
        lf = LANE_F + h
        q = qk_ref[:, :, h * ML_DQK:(h + 1) * ML_DQK]
        k = qk_ref[:, :, (ML_HEADS + h) * ML_DQK:(ML_HEADS + h + 1) * ML_DQK]
        vb = v_ref[:, :, h * ML_DV:(h + 1) * ML_DV].astype(bf16)
        qb = q.astype(bf16)
        kb = k.astype(bf16)
        d = jnp.where(causal, cs[:, :, lf:lf + 1] - rows[:, lf:lf + 1, :], NEG)
        icol = inter[:, :, lf:lf + 1]
        mt = jnp.maximum(icol, jnp.max(d, axis=2, keepdims=True))
        s = jnp.einsum('bld,bmd->blm', qb, kb, preferred_element_type=f32) * (jnp.exp(d - mt) * ML_SCALE)
        w_int = jnp.exp(icol - mt)
        c_h = c_src[:, h]
        n_h = n_src[:, h:h + 1, :]
        num = (jnp.einsum('blm,bmv->blv', s.astype(bf16), vb, preferred_element_type=f32)
               + w_int * jnp.einsum('bld,bdv->blv', qb, c_h.astype(bf16), preferred_element_type=f32))
        den = jnp.sum(s, axis=2, keepdims=True) + w_int * jnp.sum(q * n_h, axis=2, keepdims=True)
        hout = num / jnp.maximum(jnp.abs(den), jnp.exp(-mt))
        hn = (hout * lax.rsqrt(jnp.mean(hout * hout, axis=2, keepdims=True) + EPS)
              * mg_ref[:, h * ML_DV:(h + 1) * ML_DV][None])
        gate = jax.nn.sigmoid(o_ref[:, :, h * ML_DV:(h + 1) * ML_DV]) * jax.nn.silu(z_ref[:, :, h * ML_DV:(h + 1) * ML_DV])
        y_ref[:, :, h * ML_DV:(h + 1) * ML_DV] = (hn * gate).astype(bf16)
        kw = k * ws[:, :, lf:lf + 1]
        wc_h = wc[:, :, lf:lf + 1]
        c_out[:, h] = wc_h * c_h + jnp.einsum('bld,blv->bdv', kw.astype(bf16), vb, preferred_element_type=f32)
        n_out[:, h:h + 1, :] = wc_h * n_h + jnp.sum(kw, axis=1, keepdims=True)
    m_out[...] = jnp.where(is_f, m_new, 0.0)

    lpad[:, 5:8, :] = cl_src[...]
    lpad[:, 8:8 + L, :] = lx_ref[...]
    xc = lcb_ref[...][None] + lpad[:, 5:5 + L, :] * lcw_ref[0:1, :][None]
    for j in range(1, CONV_W):
        xc = xc + lpad[:, 5 + j:5 + j + L, :] * lcw_ref[j:j + 1, :][None]
    cl_out[...] = lpad[:, 5 + L:8 + L, :]
    xc2 = xc.reshape(SB * L, LRU_W)
    for p in range(LRU_BLOCKS // 2):
        sl = slice(p * 128, (p + 1) * 128)
        xp = xc2[:, sl]
        ra = jnp.dot(xp.astype(bf16), wax_ref[p], preferred_element_type=f32)
        r = jax.nn.sigmoid(ra[:, :128] + ba_ref[:, sl])
        i = jax.nn.sigmoid(ra[:, 128:] + bx_ref[:, sl])
        log_a = -LRU_C * r * jax.nn.softplus(-lam_ref[:, sl])
        a_s[:, :, sl] = jnp.exp(log_a).reshape(SB, L, 128)
        u_s[:, :, sl] = (jnp.sqrt(_neg_expm1(2.0 * log_a)) * (i * xp)).reshape(SB, L, 128)

    def lru_step(t, hcur):
        hcur = a_s[:, pl.ds(t, 1), :] * hcur + u_s[:, pl.ds(t, 1), :]
        h_s[:, pl.ds(t, 1), :] = hcur
        return hcur

    hl_out[...] = lax.fori_loop(0, L, lru_step, hl_src[...])
    y_ref[:, :, D_MODEL:2 * D_MODEL] = (h_s[...] * jax.nn.silu(lz_ref[...])).astype(bf16)

    spad[:, 5:8, :] = cs_src[...]
    spad[:, 8:8 + L, 0:COLBLK] = sx_ref[...]
    spad[:, 8:8 + L, COLBLK:2 * COLBLK] = sbc_ref[...]
    xbc = scb_ref[...][None] + spad[:, 5:5 + L, :] * scw_ref[0:1, :][None]
    for j in range(1, CONV_W):
        xbc = xbc + spad[:, 5 + j:5 + j + L, :] * scw_ref[j:j + 1, :][None]
    cs_out[...] = spad[:, 5 + L:8 + L, :]
    xbc = jax.nn.silu(xbc)
    cl = cs[:, L - 1:L, :]
    wsd = jnp.exp(cl - cs) * dt
    ecum = jnp.exp(cs)
    ecl = jnp.exp(cl)
    n_in_grp = SSD_HEADS // SSD_GROUPS
    for gi in range(SSD_GROUPS):
        bg = xbc[:, :, 1024 + gi * SSD_N:1024 + (gi + 1) * SSD_N].astype(bf16)
        cg = xbc[:, :, 1536 + gi * SSD_N:1536 + (gi + 1) * SSD_N].astype(bf16)
        gmat = jnp.einsum('bln,bmn->blm', cg, bg, preferred_element_type=f32)
        for e in range(n_in_grp):
            hd = gi * n_in_grp + e
            ln = LANE_DT + hd
            seg = jnp.where(causal, cs[:, :, ln:ln + 1] - rows[:, ln:ln + 1, :], NEG)
            w = gmat * jnp.exp(seg) * dt_rows[:, ln:ln + 1, :]
            xh = xbc[:, :, hd * SSD_P:(hd + 1) * SSD_P]
            st = hs_src[:, hd]
            y = (jnp.einsum('blm,bmp->blp', w.astype(bf16), xh.astype(bf16), preferred_element_type=f32)
                 + jnp.einsum('bln,bpn->blp', cg, st.astype(bf16), preferred_element_type=f32) * ecum[:, :, ln:ln + 1])
            ys_s[:, :, hd * SSD_P:(hd + 1) * SSD_P] = y + sd_ref[:, hd * SSD_P:(hd + 1) * SSD_P][None] * xh
            wsx = xh * wsd[:, :, ln:ln + 1]
            hs_out[:, hd] = ecl[:, :, ln:ln + 1] * st + jnp.einsum('blp,bln->bpn', wsx.astype(bf16), bg, preferred_element_type=f32)
    ysz = ys_s[...] * jax.nn.silu(sz_ref[...])
    y_ref[:, :, 2 * D_MODEL:3 * D_MODEL] = (
        ysz * lax.rsqrt(jnp.mean(ysz * ysz, axis=2, keepdims=True) + EPS) * sg_ref[...][None]).astype(bf16)


def _full_spec(arr):
    nd = arr.ndim
    return pl.BlockSpec(arr.shape, lambda i, c, _nd=nd: (0,) * _nd)


def _mixer(u_main, u_small, states, params, L, SB):
    B, T, _ = u_main.shape
    NC = T // L

    def state_spec(arr, sb):
        nd = arr.ndim
        if arr.shape[0] == B:
            return pl.BlockSpec((sb,) + arr.shape[1:], lambda i, c, _nd=nd: (i,) + (0,) * (_nd - 1))
        return pl.BlockSpec((1,) + arr.shape[1:], lambda i, c, _nd=nd: (0,) * _nd)

    in_specs = [pl.BlockSpec((SB, L, COLBLK), lambda i, c, _b=b: (i, c, _b)) for b in range(9)]
    in_specs.append(pl.BlockSpec((SB, L, N_SMALL), lambda i, c: (i, c, 0)))
    in_specs += [state_spec(s, SB) for s in states]
    in_specs += [_full_spec(p) for p in params]
    out_state_shapes = [(B,) + s.shape[1:] for s in states]
    out_specs = [pl.BlockSpec((SB, L, 3 * D_MODEL), lambda i, c: (i, c, 0))]
    out_specs += [pl.BlockSpec((SB,) + shp[1:], lambda i, c, _nd=len(shp): (i,) + (0,) * (_nd - 1))
                  for shp in out_state_shapes]
    out_shape = [jax.ShapeDtypeStruct((B, T, 3 * D_MODEL), bf16)]
    out_shape += [jax.ShapeDtypeStruct(shp, f32) for shp in out_state_shapes]
    scratch = [
        pltpu.VMEM((SB, L + 8, LRU_W), f32),
        pltpu.VMEM((SB, L + 8, SSD_CONV_CH), f32),
        pltpu.VMEM((SB, L, LRU_W), f32),
        pltpu.VMEM((SB, L, LRU_W), f32),
        pltpu.VMEM((SB, L, LRU_W), f32),
        pltpu.VMEM((SB, L, D_MODEL), f32),
    ]
    outs = pl.pallas_call(
        functools.partial(_mixer_kernel, L, SB, NC),
        grid=(B // SB, NC),
        in_specs=in_specs,
        out_specs=out_specs,
        out_shape=out_shape,
        scratch_shapes=scratch,
        compiler_params=_cparams(("arbitrary", "arbitrary")),
        name="mixer",
    )(*([u_main] * 9), u_small, *states, *params)
    return outs[0], tuple(outs[1:])


def _outproj_kernel(final, y_ref, gt_ref, x_ref, wml_ref, wlru_ref, wssd_ref, wout_ref, fg_ref, o_ref):
    gt = jax.nn.sigmoid(gt_ref[...])
    merged = gt[:, 0:D_MODEL] * jnp.dot(y_ref[:, 0:D_MODEL], wml_ref[...], preferred_element_type=f32)
    merged = merged + gt[:, D_MODEL:2 * D_MODEL] * jnp.dot(y_ref[:, D_MODEL:2 * D_MODEL], wlru_ref[...], preferred_element_type=f32)
    merged = merged + gt[:, 2 * D_MODEL:] * jnp.dot(y_ref[:, 2 * D_MODEL:], wssd_ref[...], preferred_element_type=f32)
    xn = x_ref[...] + jnp.dot(merged.astype(bf16), wout_ref[...], preferred_element_type=f32)
    if final:
        xn = xn * lax.rsqrt(jnp.mean(xn * xn, axis=-1, keepdims=True) + EPS) * fg_ref[...]
    o_ref[...] = xn


def _outproj(y2d, u_main2d, x2d, wml, wlru, wssd, wout, fg, final):
    t = x2d.shape[0]
    tm = min(t, 256)
    wspec = pl.BlockSpec((D_MODEL, D_MODEL), lambda i: (0, 0))
    return pl.pallas_call(
        functools.partial(_outproj_kernel, final),
        grid=(t // tm,),
        in_specs=[
            pl.BlockSpec((tm, 3 * D_MODEL), lambda i: (i, 0)),
            pl.BlockSpec((tm, N_GATE), lambda i: (i, N_MIX // N_GATE)),
            pl.BlockSpec((tm, D_MODEL), lambda i: (i, 0)),
            wspec, wspec, wspec, wspec,
            pl.BlockSpec((1, D_MODEL), lambda i: (0, 0)),
        ],
        out_specs=pl.BlockSpec((tm, D_MODEL), lambda i: (i, 0)),
        out_shape=jax.ShapeDtypeStruct((t, D_MODEL), f32),
        compiler_params=_cparams(("arbitrary",)),
        name="outproj",
    )(y2d, u_main2d, x2d, wml, wlru, wssd, wout, fg)


def _lane_pad(v, offset):
    n = v.shape[-1]
    pad = [(0, 0)] * (v.ndim - 1) + [(offset, N_SMALL - offset - n)]
    return jnp.pad(v, pad)


def kernel(x_prompt, x_sample, state_mlstm_c, state_mlstm_n, state_mlstm_m, state_rglru_h, state_rglru_conv, state_ssd_h, state_ssd_conv, meta_tokens, w_in, norm_g, ml_f_bias, ml_norm_g, lru_conv_w, lru_conv_b, lru_w_a, lru_b_a, lru_w_x, lru_b_x, lru_lambda, ssd_conv_w, ssd_conv_b, ssd_dt_bias, ssd_a_log, ssd_d, ssd_norm_g, w_br_ml, w_br_lru, w_br_ssd, w_out, final_norm_g):
    bp, seq, _ = x_prompt.shape
    bs, dseq, _ = x_sample.shape

    o_i = 2 * ML_HEADS * ML_DQK + ML_HEADS * ML_DV
    o_o = o_i + 2 * ML_HEADS
    o_dt = o_o + N_MIX - o_i
    o_g = o_dt + SSD_HEADS
    w_main = jnp.concatenate([w_in[:, :, :o_i], w_in[:, :, o_o:o_dt], w_in[:, :, o_g:]], axis=2).astype(bf16)
    w_small = jnp.concatenate(
        [w_in[:, :, o_i:o_o], w_in[:, :, o_dt:o_g],
         jnp.zeros((DEPTH, D_MODEL, N_SMALL - 2 * ML_HEADS - SSD_HEADS), w_in.dtype)], axis=2).astype(bf16)
    wa = lru_w_a.reshape(DEPTH, LRU_BLOCKS // 2, 2, LRU_BD, LRU_BD)
    wx = lru_w_x.reshape(DEPTH, LRU_BLOCKS // 2, 2, LRU_BD, LRU_BD)
    zb = jnp.zeros_like(wa[:, :, 0])

    def pair(w):
        top = jnp.concatenate([w[:, :, 0], zb], axis=-1)
        bot = jnp.concatenate([zb, w[:, :, 1]], axis=-1)
        return jnp.concatenate([top, bot], axis=-2)

    w_ax = jnp.concatenate([pair(wa), pair(wx)], axis=-1).astype(bf16)
    fb_pad = _lane_pad(ml_f_bias, LANE_F)[:, None, :]
    dtb_pad = _lane_pad(ssd_dt_bias, LANE_DT)[:, None, :]
    alog_pad = _lane_pad(ssd_a_log, LANE_DT)[:, None, :]
    sd_full = jnp.repeat(ssd_d, SSD_P, axis=1)[:, None, :]
    wml, wlru, wssd, wo = (w.astype(bf16) for w in (w_br_ml, w_br_lru, w_br_ssd, w_out))
    fg = final_norm_g[None, :]

    def layer_params(l):
        return (fb_pad[l], ml_norm_g[l].reshape(1, ML_HEADS * ML_DV), lru_conv_w[l], lru_conv_b[l][None],
                w_ax[l], lru_b_a[l][None], lru_b_x[l][None], lru_lambda[l][None],
                ssd_conv_w[l], ssd_conv_b[l][None], dtb_pad[l], alog_pad[l], sd_full[l], ssd_norm_g[l][None])

    def zero_states(b):
        return (jnp.zeros((b, ML_HEADS, ML_DQK, ML_DV), f32), jnp.zeros((b, ML_HEADS, ML_DQK), f32),
                jnp.zeros((b, 1, N_SMALL), f32), jnp.zeros((b, 1, LRU_W), f32),
                jnp.zeros((b, CONV_W - 1, LRU_W), f32), jnp.zeros((b, SSD_HEADS, SSD_P, SSD_N), f32),
                jnp.zeros((b, CONV_W - 1, SSD_CONV_CH), f32))

    def run_layer(l, x3, states, L, SB, need_x):
        b, t, _ = x3.shape
        x2 = x3.reshape(b * t, D_MODEL)
        u_main, u_small = _proj(x2, norm_g[l][None], w_main[l], w_small[l])
        y, new_states = _mixer(u_main.reshape(b, t, N_MAIN), u_small.reshape(b, t, N_SMALL),
                               states, layer_params(l), L, SB)
        if not need_x:
            return None, new_states
        x_new = _outproj(y.reshape(b * t, 3 * D_MODEL), u_main, x2, wml[l], wlru[l], wssd[l], wo[l], fg,
                         final=(l == DEPTH - 1))
        return x_new.reshape(b, t, D_MODEL), new_states

    def unpack(st):
        c, n, m, hl, cl, hs, cs = st
        return (c, n, m[:, 0, LANE_F:LANE_F + ML_HEADS], hl[:, 0, :], cl, hs, cs)

    x_meta = meta_tokens[None]
    x_main = x_prompt
    x_samp = x_sample
    p_states, s_states = [], []
    for l in range(DEPTH):
        last = l == DEPTH - 1
        x_meta, st_meta = run_layer(l, x_meta, zero_states(1), N_META, 1, need_x=not last)
        x_main, st_main = run_layer(l, x_main, st_meta, 128, 2, need_x=True)
        p_states.append(unpack(st_main))
        st_in = (state_mlstm_c[l], state_mlstm_n[l], _lane_pad(state_mlstm_m[l], LANE_F)[:, None, :],
                 state_rglru_h[l][:, None, :], state_rglru_conv[l], state_ssd_h[l], state_ssd_conv[l])
        x_samp, st_samp = run_layer(l, x_samp, st_in, dseq, 8, need_x=True)
        s_states.append(unpack(st_samp))

    p_stack = tuple(jnp.stack([p_states[l][j] for l in range(DEPTH)]) for j in range(7))
    s_stack = tuple(jnp.stack([s_states[l][j] for l in range(DEPTH)]) for j in range(7))
    return (x_main, x_samp) + p_stack + s_stack
```

```python
import functools

import jax
import jax.numpy as jnp
from jax import lax
from jax.experimental import pallas as pl
from jax.experimental.pallas import tpu as pltpu

f32 = jnp.float32
bf16 = jnp.bfloat16

D_MODEL = 1024
DEPTH = 2
N_META = 16
EPS = 1e-6
CONV_W = 4
ML_HEADS = 4
ML_DQK = 128
ML_DV = 256
ML_SCALE = ML_DQK ** -0.5
LRU_W = 1024
LRU_BLOCKS = 16
LRU_BD = 64
LRU_C = 8.0
SSD_HEADS = 16
SSD_P = 64
SSD_GROUPS = 4
SSD_N = 128
SSD_CONV_CH = 2048
SSD_GW = (SSD_HEADS // SSD_GROUPS) * SSD_P

COLBLK = 1024
N_MIX = 9 * COLBLK
N_GATE = 3 * D_MODEL
N_MAIN = N_MIX + N_GATE
N_SMALL = 128
LANE_I, LANE_F, LANE_DT = 0, 4, 8
SUBLANES = 8

NEG = -1e30
VMEM_LIMIT = 56 * 1024 * 1024
N_STATE = 7


def _cparams(sem):
    return pltpu.CompilerParams(dimension_semantics=sem, vmem_limit_bytes=VMEM_LIMIT)


def _proj_kernel(x_ref, g_ref, w_ref, ws_ref, u_ref, us_ref, xn_ref):
    @pl.when(pl.program_id(1) == 0)
    def _():
        x = x_ref[...]
        xn = x * lax.rsqrt(jnp.mean(x * x, axis=-1, keepdims=True) + EPS) * g_ref[...]
        xn_ref[...] = xn.astype(bf16)
        us_ref[...] = jnp.dot(xn_ref[...], ws_ref[...], preferred_element_type=f32)

    u_ref[...] = jnp.dot(xn_ref[...], w_ref[...], preferred_element_type=f32)


def _proj(x2d, g, w_main, w_small):
    t = x2d.shape[0]
    tm = min(t, 1024)
    tn = 2048
    return pl.pallas_call(
        _proj_kernel,
        grid=(t // tm, N_MAIN // tn),
        in_specs=[
            pl.BlockSpec((tm, D_MODEL), lambda i, j: (i, 0)),
            pl.BlockSpec((1, D_MODEL), lambda i, j: (0, 0)),
            pl.BlockSpec((D_MODEL, tn), lambda i, j: (0, j)),
            pl.BlockSpec((D_MODEL, N_SMALL), lambda i, j: (0, 0)),
        ],
        out_specs=[
            pl.BlockSpec((tm, tn), lambda i, j: (i, j)),
            pl.BlockSpec((tm, N_SMALL), lambda i, j: (i, 0)),
        ],
        out_shape=[
            jax.ShapeDtypeStruct((t, N_MAIN), f32),
            jax.ShapeDtypeStruct((t, N_SMALL), f32),
        ],
        scratch_shapes=[pltpu.VMEM((tm, D_MODEL), bf16)],
        compiler_params=_cparams(("arbitrary", "arbitrary")),
        name="proj",
    )(x2d, g, w_main, w_small)


def _cumsum_rows(x3, seg_len):
    sb = x3.shape[0]
    x = x3.reshape(sb * seg_len, x3.shape[2])
    t = lax.broadcasted_iota(jnp.int32, x.shape, 0) & (seg_len - 1)
    s = 1
    while s < seg_len:
        x = x + jnp.where(t >= s, pltpu.roll(x, s, 0), 0.0)
        s *= 2
    return x.reshape(x3.shape)


def _neg_expm1(x):
    u = jnp.exp(x)
    return jnp.where(u == 1.0, -x, (1.0 - u) * x / jnp.log(u))


def _causal_conv(pad_ref, carry_ref, x_parts, w_ref, b_ref, L):
    sb, _, c = pad_ref.shape
    pad_ref[:, 0:SUBLANES, :] = jnp.zeros((sb, SUBLANES, c), f32)
    pad_ref[:, SUBLANES - (CONV_W - 1):SUBLANES, :] = carry_ref[...]
    col = 0
    for xp in x_parts:
        pad_ref[:, SUBLANES:SUBLANES + L, col:col + xp.shape[2]] = xp
        col += xp.shape[2]
    xe = pad_ref[...]
    xe2 = xe.reshape(sb * (SUBLANES + L), c)
    y = b_ref[...][None]
    for j in range(CONV_W - 1):
        tap = pltpu.roll(xe2, CONV_W - 1 - j, 0).reshape(sb, SUBLANES + L, c)[:, SUBLANES:, :]
        y = y + tap * w_ref[j:j + 1, :][None]
    y = y + xe[:, SUBLANES:, :] * w_ref[CONV_W - 1:CONV_W, :][None]
    return y, pad_ref[:, SUBLANES + L - (CONV_W - 1):SUBLANES + L, :]


def _mixer_kernel(L, SB, NC, n_alias, *refs):
    (qk_ref, v_ref, o_ref, z_ref, lx_ref, lz_ref, sz_ref, sx_ref, sbc_ref, sm_ref,
     c_in, n_in, m_in, hl_in, cl_in, hs_in, cs_in,
     fb_ref, mg_ref, lcw_ref, lcb_ref, wax_ref, ba_ref, bx_ref, lam_ref,
     scw_ref, scb_ref, dtb_ref, alog_ref, sd_ref, sg_ref) = refs[:31]
    (y_ref, c_out, n_out, m_out, hl_out, cl_out, hs_out, cs_out,
     lpad, spad, a_s, u_s, h_s, ys_s) = refs[31 + n_alias:]

    if NC == 1:
        c_src, n_src, m_src, hl_src, cl_src, hs_src, cs_src = c_in, n_in, m_in, hl_in, cl_in, hs_in, cs_in
    else:
        c_src, n_src, m_src, hl_src, cl_src, hs_src, cs_src = c_out, n_out, m_out, hl_out, cl_out, hs_out, cs_out

        @pl.when(pl.program_id(1) == 0)
        def _():
            for src, dst in ((c_in, c_out), (n_in, n_out), (m_in, m_out), (hl_in, hl_out),
                             (cl_in, cl_out), (hs_in, hs_out), (cs_in, cs_out)):
                dst[...] = jnp.broadcast_to(src[...], dst.shape)

    lane = lax.broadcasted_iota(jnp.int32, (1, 1, N_SMALL), 2)
    is_f = jnp.abs(2 * lane - (2 * LANE_F + ML_HEADS - 1)) < ML_HEADS
    is_dt = jnp.abs(2 * lane - (2 * LANE_DT + SSD_HEADS - 1)) < SSD_HEADS
    ri = lax.broadcasted_iota(jnp.int32, (1, L, L), 1)
    ci = lax.broadcasted_iota(jnp.int32, (1, L, L), 2)
    causal = ri >= ci

    sm = sm_ref[...]
    logf = jax.nn.log_sigmoid(sm + fb_ref[...][None])
    dt = jax.nn.softplus(sm + dtb_ref[...][None])
    a_neg = -jnp.exp(alog_ref[...])[None]
    cs = _cumsum_rows(jnp.where(is_f, logf, jnp.where(is_dt, dt * a_neg, 0.0)), L)
    li4 = pltpu.roll(sm.reshape(SB * L, N_SMALL), LANE_F - LANE_I, 1).reshape(SB, L, N_SMALL)

    m0 = m_src[...]
    bl = cs[:, L - 1:L, :]
    inter = cs + m0
    g = bl - cs + li4
    m_new = jnp.maximum(bl + m0, jnp.max(g, axis=1, keepdims=True))
    ws = jnp.exp(g - m_new) * ML_SCALE
    wc = jnp.exp(bl + m0 - m_new)
    rows = jnp.swapaxes(jnp.where(is_f, cs - li4, cs), 1, 2)
    dt_rows = jnp.swapaxes(dt, 1, 2)

    for h in range(ML_HEADS):
        lf = LANE_F + h
        q = qk_ref[:, :, h * ML_DQK:(h + 1) * ML_DQK]
        k = qk_ref[:, :, (ML_HEADS + h) * ML_DQK:(ML_HEADS + h + 1) * ML_DQK]
        vb = v_ref[:, :, h * ML_DV:(h + 1) * ML_DV].astype(bf16)
        qb = q.astype(bf16)
        kb = k.astype(bf16)
        d = jnp.where(causal, cs[:, :, lf:lf + 1] - rows[:, lf:lf + 1, :], NEG)
        icol = inter[:, :, lf:lf + 1]
        mt = jnp.maximum(icol, jnp.max(d, axis=2, keepdims=True))
        s = jnp.einsum('bld,bmd->blm', qb, kb, preferred_element_type=f32) * (jnp.exp(d - mt) * ML_SCALE)
        w_int = jnp.exp(icol - mt)
        c_h = c_src[:, h]
        n_h = n_src[:, h:h + 1, :]
        num = (jnp.einsum('blm,bmv->blv', s.astype(bf16), vb, preferred_element_type=f32)
               + w_int * jnp.einsum('bld,bdv->blv', qb, c_h.astype(bf16), preferred_element_type=f32))
        den = jnp.sum(s, axis=2, keepdims=True) + w_int * jnp.sum(q * n_h, axis=2, keepdims=True)
        hout = num / jnp.maximum(jnp.abs(den), jnp.exp(-mt))
        hn = (hout * lax.rsqrt(jnp.mean(hout * hout, axis=2, keepdims=True) + EPS)
              * mg_ref[:, h * ML_DV:(h + 1) * ML_DV][None])
        gate = jax.nn.sigmoid(o_ref[:, :, h * ML_DV:(h + 1) * ML_DV]) * jax.nn.silu(z_ref[:, :, h * ML_DV:(h + 1) * ML_DV])
        y_ref[:, :, h * ML_DV:(h + 1) * ML_DV] = (hn * gate).astype(bf16)
        kw = k * ws[:, :, lf:lf + 1]
        wc_h = wc[:, :, lf:lf + 1]
        c_out[:, h] = wc_h * c_h + jnp.einsum('bld,blv->bdv', kw.astype(bf16), vb, preferred_element_type=f32)
        n_out[:, h:h + 1, :] = wc_h * n_h + jnp.sum(kw, axis=1, keepdims=True)
    m_out[...] = jnp.where(is_f, m_new, 0.0)

    xc, cl_new = _causal_conv(lpad, cl_src, (lx_ref[...],), lcw_ref, lcb_ref, L)
    cl_out[...] = cl_new
    xc2 = xc.reshape(SB * L, LRU_W)
    for p in range(LRU_BLOCKS // 2):
        sl = slice(p * 128, (p + 1) * 128)
        xp = xc2[:, sl]
        ra = jnp.dot(xp.astype(bf16), wax_ref[p], preferred_element_type=f32)
        r = jax.nn.sigmoid(ra[:, :128] + ba_ref[:, sl])
        i = jax.nn.sigmoid(ra[:, 128:] + bx_ref[:, sl])
        log_a = -LRU_C * r * jax.nn.softplus(-lam_ref[:, sl])
        a_s[:, :, sl] = jnp.exp(log_a).reshape(SB, L, 128)
        u_s[:, :, sl] = (jnp.sqrt(_neg_expm1(2.0 * log_a)) * (i * xp)).reshape(SB, L, 128)

    def lru_step(t, hcur):
        hcur = a_s[:, pl.ds(t, 1), :] * hcur + u_s[:, pl.ds(t, 1), :]
        h_s[:, pl.ds(t, 1), :] = hcur
        return hcur

    hl_out[...] = lax.fori_loop(0, L, lru_step, hl_src[...])
    y_ref[:, :, D_MODEL:2 * D_MODEL] = (h_s[...] * jax.nn.silu(lz_ref[...])).astype(bf16)

    xbc, cs_new = _causal_conv(spad, cs_src, (sx_ref[...], sbc_ref[...]), scw_ref, scb_ref, L)
    cs_out[...] = cs_new
    xbc = jax.nn.silu(xbc)
    cl = cs[:, L - 1:L, :]
    wsd = jnp.exp(cl - cs) * dt
    ecum = jnp.exp(cs)
    ecl = jnp.exp(cl)
    n_in_grp = SSD_HEADS // SSD_GROUPS
    for gi in range(SSD_GROUPS):
        bg = xbc[:, :, 1024 + gi * SSD_N:1024 + (gi + 1) * SSD_N].astype(bf16)
        cg = xbc[:, :, 1536 + gi * SSD_N:1536 + (gi + 1) * SSD_N].astype(bf16)
        gmat = jnp.einsum('bln,bmn->blm', cg, bg, preferred_element_type=f32)
        heads = [gi * n_in_grp + e for e in range(n_in_grp)]
        ws_l = []
        for hd in heads:
            ln = LANE_DT + hd
            seg = jnp.where(causal, cs[:, :, ln:ln + 1] - rows[:, ln:ln + 1, :], NEG)
            ws_l.append((gmat * jnp.exp(seg) * dt_rows[:, ln:ln + 1, :]).astype(bf16))
        if L % 128 == 0:
            gsl = slice(gi * SSD_GW, (gi + 1) * SSD_GW)
            xg = xbc[:, :, gsl]
            xgb = xg.astype(bf16)
            blk = lax.broadcasted_iota(jnp.int32, (1, 1, SSD_GW), 2) // SSD_P

            def per_head(cols):
                out = cols[n_in_grp - 1]
                for e in range(n_in_grp - 2, -1, -1):
                    out = jnp.where(blk == e, cols[e], out)
                return out

            xbd = jnp.concatenate([jnp.where(blk == e, xgb, jnp.zeros_like(xgb)) for e in range(n_in_grp)], axis=1)
            st = hs_src[:, heads[0]:heads[0] + n_in_grp].reshape(SB, SSD_GW, SSD_N)
            y = (jnp.einsum('blm,bmp->blp', jnp.concatenate(ws_l, axis=2), xbd, preferred_element_type=f32)
                 + jnp.einsum('bln,bpn->blp', cg, st.astype(bf16), preferred_element_type=f32)
                 * per_head([ecum[:, :, LANE_DT + hd:LANE_DT + hd + 1] for hd in heads]))
            ys_s[:, :, gsl] = y + sd_ref[:, gsl][None] * xg
            wsx = xg * per_head([wsd[:, :, LANE_DT + hd:LANE_DT + hd + 1] for hd in heads])
            upd = jnp.einsum('blp,bln->bpn', wsx.astype(bf16), bg, preferred_element_type=f32)
            for e, hd in enumerate(heads):
                ln = LANE_DT + hd
                hs_out[:, hd] = ecl[:, :, ln:ln + 1] * st[:, e * SSD_P:(e + 1) * SSD_P, :] + upd[:, e * SSD_P:(e + 1) * SSD_P, :]
        else:
            for e, hd in enumerate(heads):
                ln = LANE_DT + hd
                xh = xbc[:, :, hd * SSD_P:(hd + 1) * SSD_P]
                st = hs_src[:, hd]
                y = (jnp.einsum('blm,bmp->blp', ws_l[e], xh.astype(bf16), preferred_element_type=f32)
                     + jnp.einsum('bln,bpn->blp', cg, st.astype(bf16), preferred_element_type=f32) * ecum[:, :, ln:ln + 1])
                ys_s[:, :, hd * SSD_P:(hd + 1) * SSD_P] = y + sd_ref[:, hd * SSD_P:(hd + 1) * SSD_P][None] * xh
                wsx = xh * wsd[:, :, ln:ln + 1]
                hs_out[:, hd] = ecl[:, :, ln:ln + 1] * st + jnp.einsum('blp,bln->bpn', wsx.astype(bf16), bg, preferred_element_type=f32)
    ysz = ys_s[...] * jax.nn.silu(sz_ref[...])
    y_ref[:, :, 2 * D_MODEL:3 * D_MODEL] = (
        ysz * lax.rsqrt(jnp.mean(ysz * ysz, axis=2, keepdims=True) + EPS) * sg_ref[...][None]).astype(bf16)


def _full_spec(arr):
    nd = arr.ndim
    return pl.BlockSpec(arr.shape, lambda i, c, _nd=nd: (0,) * _nd)


def _mixer(u_main, u_small, states, lin, prev_out, lout, params, L, SB):
    B, T, _ = u_main.shape
    NC = T // L

    def state_spec(arr):
        rest = arr.shape[2:]
        nz = (0,) * len(rest)
        if arr.shape[1] == B:
            return pl.BlockSpec((None, SB) + rest, lambda i, c: (lin, i) + nz)
        return pl.BlockSpec((None, 1) + rest, lambda i, c: (lin, 0) + nz)

    in_specs = [pl.BlockSpec((SB, L, COLBLK), lambda i, c, _b=b: (i, c, _b)) for b in range(N_MIX // COLBLK)]
    in_specs.append(pl.BlockSpec((SB, L, N_SMALL), lambda i, c: (i, c, 0)))
    in_specs += [state_spec(s) for s in states]
    in_specs += [_full_spec(p) for p in params]
    n_alias = 0 if prev_out is None else N_STATE
    aliases = {}
    if prev_out is not None:
        first = len(in_specs)
        in_specs += [pl.BlockSpec(memory_space=pl.ANY)] * N_STATE
        aliases = {first + j: 1 + j for j in range(N_STATE)}
    out_state_shapes = [(DEPTH, B) + s.shape[2:] for s in states]
    out_specs = [pl.BlockSpec((SB, L, 3 * D_MODEL), lambda i, c: (i, c, 0))]
    out_specs += [pl.BlockSpec((None, SB) + shp[2:], lambda i, c, _n=len(shp) - 2: (lout, i) + (0,) * _n)
                  for shp in out_state_shapes]
    out_shape = [jax.ShapeDtypeStruct((B, T, 3 * D_MODEL), bf16)]
    out_shape += [jax.ShapeDtypeStruct(shp, f32) for shp in out_state_shapes]
    scratch = [
        pltpu.VMEM((SB, SUBLANES + L, LRU_W), f32),
        pltpu.VMEM((SB, SUBLANES + L, SSD_CONV_CH), f32),
        pltpu.VMEM((SB, L, LRU_W), f32),
        pltpu.VMEM((SB, L, LRU_W), f32),
        pltpu.VMEM((SB, L, LRU_W), f32),
        pltpu.VMEM((SB, L, D_MODEL), f32),
    ]
    extra = () if prev_out is None else tuple(prev_out)
    outs = pl.pallas_call(
        functools.partial(_mixer_kernel, L, SB, NC, n_alias),
        grid=(B // SB, NC),
        in_specs=in_specs,
        out_specs=out_specs,
        out_shape=out_shape,
        scratch_shapes=scratch,
        input_output_aliases=aliases,
        compiler_params=_cparams(("arbitrary", "arbitrary")),
        name="mixer",
    )(*([u_main] * (N_MIX // COLBLK)), u_small, *states, *params, *extra)
    return outs[0], tuple(outs[1:])


def _outproj_kernel(final, y_ref, gt_ref, x_ref, wml_ref, wlru_ref, wssd_ref, wout_ref, fg_ref, o_ref):
    gt = jax.nn.sigmoid(gt_ref[...])
    merged = gt[:, 0:D_MODEL] * jnp.dot(y_ref[:, 0:D_MODEL], wml_ref[...], preferred_element_type=f32)
    merged = merged + gt[:, D_MODEL:2 * D_MODEL] * jnp.dot(y_ref[:, D_MODEL:2 * D_MODEL], wlru_ref[...], preferred_element_type=f32)
    merged = merged + gt[:, 2 * D_MODEL:] * jnp.dot(y_ref[:, 2 * D_MODEL:], wssd_ref[...], preferred_element_type=f32)
    xn = x_ref[...] + jnp.dot(merged.astype(bf16), wout_ref[...], preferred_element_type=f32)
    if final:
        xn = xn * lax.rsqrt(jnp.mean(xn * xn, axis=-1, keepdims=True) + EPS) * fg_ref[...]
    o_ref[...] = xn


def _outproj(y2d, u_main2d, x2d, wml, wlru, wssd, wout, fg, final):
    t = x2d.shape[0]
    tm = min(t, 256)
    wspec = pl.BlockSpec((D_MODEL, D_MODEL), lambda i: (0, 0))
    return pl.pallas_call(
        functools.partial(_outproj_kernel, final),
        grid=(t // tm,),
        in_specs=[
            pl.BlockSpec((tm, 3 * D_MODEL), lambda i: (i, 0)),
            pl.BlockSpec((tm, N_GATE), lambda i: (i, N_MIX // N_GATE)),
            pl.BlockSpec((tm, D_MODEL), lambda i: (i, 0)),
            wspec, wspec, wspec, wspec,
            pl.BlockSpec((1, D_MODEL), lambda i: (0, 0)),
        ],
        out_specs=pl.BlockSpec((tm, D_MODEL), lambda i: (i, 0)),
        out_shape=jax.ShapeDtypeStruct((t, D_MODEL), f32),
        compiler_params=_cparams(("arbitrary",)),
        name="outproj",
    )(y2d, u_main2d, x2d, wml, wlru, wssd, wout, fg)


def _lane_pad(v, offset):
    n = v.shape[-1]
    pad = [(0, 0)] * (v.ndim - 1) + [(offset, N_SMALL - offset - n)]
    return jnp.pad(v, pad)


def kernel(x_prompt, x_sample, state_mlstm_c, state_mlstm_n, state_mlstm_m, state_rglru_h, state_rglru_conv, state_ssd_h, state_ssd_conv, meta_tokens, w_in, norm_g, ml_f_bias, ml_norm_g, lru_conv_w, lru_conv_b, lru_w_a, lru_b_a, lru_w_x, lru_b_x, lru_lambda, ssd_conv_w, ssd_conv_b, ssd_dt_bias, ssd_a_log, ssd_d, ssd_norm_g, w_br_ml, w_br_lru, w_br_ssd, w_out, final_norm_g):
    dseq = x_sample.shape[1]

    o_i = 2 * ML_HEADS * ML_DQK + ML_HEADS * ML_DV
    o_o = o_i + 2 * ML_HEADS
    o_dt = o_o + N_MIX - o_i
    o_g = o_dt + SSD_HEADS
    w_main = jnp.concatenate([w_in[:, :, :o_i], w_in[:, :, o_o:o_dt], w_in[:, :, o_g:]], axis=2).astype(bf16)
    w_small = jnp.concatenate(
        [w_in[:, :, o_i:o_o], w_in[:, :, o_dt:o_g],
         jnp.zeros((DEPTH, D_MODEL, N_SMALL - 2 * ML_HEADS - SSD_HEADS), w_in.dtype)], axis=2).astype(bf16)
    wa = lru_w_a.reshape(DEPTH, LRU_BLOCKS // 2, 2, LRU_BD, LRU_BD)
    wx = lru_w_x.reshape(DEPTH, LRU_BLOCKS // 2, 2, LRU_BD, LRU_BD)
    zb = jnp.zeros_like(wa[:, :, 0])

    def pair(w):
        top = jnp.concatenate([w[:, :, 0], zb], axis=-1)
        bot = jnp.concatenate([zb, w[:, :, 1]], axis=-1)
        return jnp.concatenate([top, bot], axis=-2)

    w_ax = jnp.concatenate([pair(wa), pair(wx)], axis=-1).astype(bf16)
    fb_pad = _lane_pad(ml_f_bias, LANE_F)[:, None, :]
    dtb_pad = _lane_pad(ssd_dt_bias, LANE_DT)[:, None, :]
    alog_pad = _lane_pad(ssd_a_log, LANE_DT)[:, None, :]
    sd_full = jnp.repeat(ssd_d, SSD_P, axis=1)[:, None, :]
    wml, wlru, wssd, wo = (w.astype(bf16) for w in (w_br_ml, w_br_lru, w_br_ssd, w_out))
    fg = final_norm_g[None, :]

    def layer_params(l):
        return (fb_pad[l], ml_norm_g[l].reshape(1, ML_HEADS * ML_DV), lru_conv_w[l], lru_conv_b[l][None],
                w_ax[l], lru_b_a[l][None], lru_b_x[l][None], lru_lambda[l][None],
                ssd_conv_w[l], ssd_conv_b[l][None], dtb_pad[l], alog_pad[l], sd_full[l], ssd_norm_g[l][None])

    zero_states = (jnp.zeros((1, 1, ML_HEADS, ML_DQK, ML_DV), f32), jnp.zeros((1, 1, ML_HEADS, ML_DQK), f32),
                   jnp.zeros((1, 1, 1, N_SMALL), f32), jnp.zeros((1, 1, 1, LRU_W), f32),
                   jnp.zeros((1, 1, CONV_W - 1, LRU_W), f32), jnp.zeros((1, 1, SSD_HEADS, SSD_P, SSD_N), f32),
                   jnp.zeros((1, 1, CONV_W - 1, SSD_CONV_CH), f32))
    sample_states = (state_mlstm_c, state_mlstm_n, _lane_pad(state_mlstm_m, LANE_F)[:, :, None, :],
                     state_rglru_h[:, :, None, :], state_rglru_conv, state_ssd_h, state_ssd_conv)

    def run_layer(l, x3, states, lin, prev_out, L, SB, need_x):
        b, t, _ = x3.shape
        x2 = x3.reshape(b * t, D_MODEL)
        u_main, u_small = _proj(x2, norm_g[l][None], w_main[l], w_small[l])
        y, new_states = _mixer(u_main.reshape(b, t, N_MAIN), u_small.reshape(b, t, N_SMALL),
                               states, lin, prev_out, l, layer_params(l), L, SB)
        if not need_x:
            return None, new_states
        x_new = _outproj(y.reshape(b * t, 3 * D_MODEL), u_main, x2, wml[l], wlru[l], wssd[l], wo[l], fg,
                         final=(l == DEPTH - 1))
        return x_new.reshape(b, t, D_MODEL), new_states

    def unpack(st):
        c, n, m, hl, cl, hs, cs = st
        return (c, n, m[:, :, 0, LANE_F:LANE_F + ML_HEADS], hl[:, :, 0, :], cl, hs, cs)

    x_meta = meta_tokens[None]
    x_main = x_prompt
    x_samp = x_sample
    st_meta = st_main = st_samp = None
    for l in range(DEPTH):
        last = l == DEPTH - 1
        x_meta, st_meta = run_layer(l, x_meta, zero_states, 0, st_meta, N_META, 1, need_x=not last)
        x_main, st_main = run_layer(l, x_main, st_meta, l, st_main, 128, 2, need_x=True)
        x_samp, st_samp = run_layer(l, x_samp, sample_states, l, st_samp, dseq, 8, need_x=True)

    return (x_main, x_samp) + unpack(st_main) + unpack(st_samp)
```

```python
import functools

import jax
import jax.numpy as jnp
from jax import lax
from jax.experimental import pallas as pl
from jax.experimental.pallas import tpu as pltpu

f32 = jnp.float32
bf16 = jnp.bfloat16

D_MODEL = 1024
DEPTH = 2
N_META = 16
EPS = 1e-6
CONV_W = 4
ML_HEADS = 4
ML_DQK = 128
ML_DV = 256
ML_SCALE = ML_DQK ** -0.5
LRU_W = 1024
LRU_BLOCKS = 16
LRU_BD = 64
LRU_C = 8.0
SSD_HEADS = 16
SSD_P = 64
SSD_GROUPS = 4
SSD_N = 128
SSD_CONV_CH = 2048
SSD_GW = (SSD_HEADS // SSD_GROUPS) * SSD_P

COLBLK = 1024
N_MIX = 9 * COLBLK
N_GATE = 3 * D_MODEL
N_MAIN = N_MIX + N_GATE
N_SMALL = 128
LANE_I, LANE_F, LANE_DT = 0, 4, 8
SUBLANES = 8

NEG = -1e30
VMEM_LIMIT = 56 * 1024 * 1024
N_STATE = 7


def _cparams(sem):
    return pltpu.CompilerParams(dimension_semantics=sem, vmem_limit_bytes=VMEM_LIMIT)


def _proj_kernel(x_ref, g_ref, w_ref, ws_ref, u_ref, us_ref, xn_ref):
    @pl.when(pl.program_id(1) == 0)
    def _():
        x = x_ref[...]
        xn = x * lax.rsqrt(jnp.mean(x * x, axis=-1, keepdims=True) + EPS) * g_ref[...]
        xn_ref[...] = xn.astype(bf16)
        us_ref[...] = jnp.dot(xn_ref[...], ws_ref[...], preferred_element_type=f32)

    u_ref[...] = jnp.dot(xn_ref[...], w_ref[...], preferred_element_type=f32)


def _proj(x2d, g, w_main, w_small):
    t = x2d.shape[0]
    tm = min(t, 1024)
    tn = 2048
    return pl.pallas_call(
        _proj_kernel,
        grid=(t // tm, N_MAIN // tn),
        in_specs=[
            pl.BlockSpec((tm, D_MODEL), lambda i, j: (i, 0)),
            pl.BlockSpec((1, D_MODEL), lambda i, j: (0, 0)),
            pl.BlockSpec((D_MODEL, tn), lambda i, j: (0, j)),
            pl.BlockSpec((D_MODEL, N_SMALL), lambda i, j: (0, 0)),
        ],
        out_specs=[
            pl.BlockSpec((tm, tn), lambda i, j: (i, j)),
            pl.BlockSpec((tm, N_SMALL), lambda i, j: (i, 0)),
        ],
        out_shape=[
            jax.ShapeDtypeStruct((t, N_MAIN), f32),
            jax.ShapeDtypeStruct((t, N_SMALL), f32),
        ],
        scratch_shapes=[pltpu.VMEM((tm, D_MODEL), bf16)],
        compiler_params=_cparams(("arbitrary", "arbitrary")),
        name="proj",
    )(x2d, g, w_main, w_small)


def _cumsum_rows(x3, seg_len):
    sb = x3.shape[0]
    x = x3.reshape(sb * seg_len, x3.shape[2])
    t = lax.broadcasted_iota(jnp.int32, x.shape, 0) & (seg_len - 1)
    s = 1
    while s < seg_len:
        x = x + jnp.where(t >= s, pltpu.roll(x, s, 0), 0.0)
        s *= 2
    return x.reshape(x3.shape)


def _one_minus_sq(log_a, a):
    return jnp.tanh(-log_a) * (1.0 + a * a)


def _causal_conv(pad_ref, carry_ref, x_parts, w_ref, b_ref, L):
    lo = SUBLANES - (CONV_W - 1)
    col = 0
    for xp in x_parts:
        for s in range(xp.shape[2] // 128):
            pad_ref[col + s, :, SUBLANES:SUBLANES + L, :] = xp[:, :, s * 128:(s + 1) * 128]
        col += xp.shape[2] // 128
    outs, carry = [], []
    for c in range(pad_ref.shape[0]):
        sl = slice(c * 128, (c + 1) * 128)
        pad_ref[c, :, lo:SUBLANES, :] = carry_ref[:, :, sl]
        y = b_ref[:, sl][None]
        for j in range(CONV_W):
            y = y + pad_ref[c, :, lo + j:lo + j + L, :] * w_ref[j:j + 1, sl][None]
        outs.append(y)
        carry.append(pad_ref[c, :, lo + L:SUBLANES + L, :])
    return jnp.concatenate(outs, axis=2), jnp.concatenate(carry, axis=2)


def _mixer_kernel(L, SB, NC, n_alias, *refs):
    (qk_ref, v_ref, o_ref, z_ref, lx_ref, lz_ref, sz_ref, sx_ref, sbc_ref, sm_ref,
     c_in, n_in, m_in, hl_in, cl_in, hs_in, cs_in,
     fb_ref, mg_ref, lcw_ref, lcb_ref, wax_ref, ba_ref, bx_ref, lam_ref,
     scw_ref, scb_ref, dtb_ref, alog_ref, sd_ref, sg_ref) = refs[:31]
    (y_ref, c_out, n_out, m_out, hl_out, cl_out, hs_out, cs_out,
     lpad, spad, a_s, u_s, h_s, ys_s) = refs[31 + n_alias:]

    if NC == 1:
        c_src, n_src, m_src, hl_src, cl_src, hs_src, cs_src = c_in, n_in, m_in, hl_in, cl_in, hs_in, cs_in
    else:
        c_src, n_src, m_src, hl_src, cl_src, hs_src, cs_src = c_out, n_out, m_out, hl_out, cl_out, hs_out, cs_out

        @pl.when(pl.program_id(1) == 0)
        def _():
            for src, dst in ((c_in, c_out), (n_in, n_out), (m_in, m_out), (hl_in, hl_out),
                             (cl_in, cl_out), (hs_in, hs_out), (cs_in, cs_out)):
                dst[...] = jnp.broadcast_to(src[...], dst.shape)

    lane = lax.broadcasted_iota(jnp.int32, (1, 1, N_SMALL), 2)
    is_f = jnp.abs(2 * lane - (2 * LANE_F + ML_HEADS - 1)) < ML_HEADS
    is_dt = jnp.abs(2 * lane - (2 * LANE_DT + SSD_HEADS - 1)) < SSD_HEADS
    ri = lax.broadcasted_iota(jnp.int32, (1, L, L), 1)
    ci = lax.broadcasted_iota(jnp.int32, (1, L, L), 2)
    causal = ri >= ci

    sm = sm_ref[...]
    logf = jax.nn.log_sigmoid(sm + fb_ref[...][None])
    dt = jax.nn.softplus(sm + dtb_ref[...][None])
    a_neg = -jnp.exp(alog_ref[...])[None]
    cs = _cumsum_rows(jnp.where(is_f, logf, jnp.where(is_dt, dt * a_neg, 0.0)), L)
    li4 = pltpu.roll(sm.reshape(SB * L, N_SMALL), LANE_F - LANE_I, 1).reshape(SB, L, N_SMALL)

    m0 = m_src[...]
    bl = cs[:, L - 1:L, :]
    inter = cs + m0
    g = bl - cs + li4
    m_new = jnp.maximum(bl + m0, jnp.max(g, axis=1, keepdims=True))
    ws = jnp.exp(g - m_new) * ML_SCALE
    wc = jnp.exp(bl + m0 - m_new)
    rows = jnp.swapaxes(jnp.where(is_f, cs - li4, cs), 1, 2)
    dt_rows = jnp.swapaxes(dt, 1, 2)

    for h in range(ML_HEADS):
        lf = LANE_F + h
        q = qk_ref[:, :, h * ML_DQK:(h + 1) * ML_DQK]
        k = qk_ref[:, :, (ML_HEADS + h) * ML_DQK:(ML_HEADS + h + 1) * ML_DQK]
        vb = v_ref[:, :, h * ML_DV:(h + 1) * ML_DV].astype(bf16)
        qb = q.astype(bf16)
        kb = k.astype(bf16)
        d = jnp.where(causal, cs[:, :, lf:lf + 1] - rows[:, lf:lf + 1, :], NEG)
        icol = inter[:, :, lf:lf + 1]
        mt = jnp.maximum(icol, jnp.max(d, axis=2, keepdims=True))
        s = jnp.einsum('bld,bmd->blm', qb, kb, preferred_element_type=f32) * (jnp.exp(d - mt) * ML_SCALE)
        w_int = jnp.exp(icol - mt)
        c_h = c_src[:, h]
        n_h = n_src[:, h:h + 1, :]
        num = (jnp.einsum('blm,bmv->blv', s.astype(bf16), vb, preferred_element_type=f32)
               + w_int * jnp.einsum('bld,bdv->blv', qb, c_h.astype(bf16), preferred_element_type=f32))
        den = jnp.sum(s, axis=2, keepdims=True) + w_int * jnp.sum(q * n_h, axis=2, keepdims=True)
        hout = num / jnp.maximum(jnp.abs(den), jnp.exp(-mt))
        hn = (hout * lax.rsqrt(jnp.mean(hout * hout, axis=2, keepdims=True) + EPS)
              * mg_ref[:, h * ML_DV:(h + 1) * ML_DV][None])
        gate = jax.nn.sigmoid(o_ref[:, :, h * ML_DV:(h + 1) * ML_DV]) * jax.nn.silu(z_ref[:, :, h * ML_DV:(h + 1) * ML_DV])
        y_ref[:, :, h * ML_DV:(h + 1) * ML_DV] = (hn * gate).astype(bf16)
        kw = k * ws[:, :, lf:lf + 1]
        wc_h = wc[:, :, lf:lf + 1]
        c_out[:, h] = wc_h * c_h + jnp.einsum('bld,blv->bdv', kw.astype(bf16), vb, preferred_element_type=f32)
        n_out[:, h:h + 1, :] = wc_h * n_h + jnp.sum(kw, axis=1, keepdims=True)
    m_out[...] = jnp.where(is_f, m_new, 0.0)

    xc, cl_new = _causal_conv(lpad, cl_src, (lx_ref[...],), lcw_ref, lcb_ref, L)
    cl_out[...] = cl_new
    xc2 = xc.reshape(SB * L, LRU_W)
    for p in range(LRU_BLOCKS // 2):
        sl = slice(p * 128, (p + 1) * 128)
        xp = xc2[:, sl]
        ra = jnp.dot(xp.astype(bf16), wax_ref[p], preferred_element_type=f32)
        r = jax.nn.sigmoid(ra[:, :128] + ba_ref[:, sl])
        i = jax.nn.sigmoid(ra[:, 128:] + bx_ref[:, sl])
        log_a = -LRU_C * r * jax.nn.softplus(-lam_ref[:, sl])
        a = jnp.exp(log_a)
        a_s[:, :, sl] = a.reshape(SB, L, 128)
        u_s[:, :, sl] = (jnp.sqrt(_one_minus_sq(log_a, a)) * (i * xp)).reshape(SB, L, 128)

    def lru_step(t, hcur):
        hcur = a_s[:, pl.ds(t, 1), :] * hcur + u_s[:, pl.ds(t, 1), :]
        h_s[:, pl.ds(t, 1), :] = hcur
        return hcur

    hl_out[...] = lax.fori_loop(0, L, lru_step, hl_src[...])
    y_ref[:, :, D_MODEL:2 * D_MODEL] = (h_s[...] * jax.nn.silu(lz_ref[...])).astype(bf16)

    xbc, cs_new = _causal_conv(spad, cs_src, (sx_ref[...], sbc_ref[...]), scw_ref, scb_ref, L)
    cs_out[...] = cs_new
    xbc = jax.nn.silu(xbc)
    cl = cs[:, L - 1:L, :]
    wsd = jnp.exp(cl - cs) * dt
    ecum = jnp.exp(cs)
    ecl = jnp.exp(cl)
    n_in_grp = SSD_HEADS // SSD_GROUPS
    for gi in range(SSD_GROUPS):
        bg = xbc[:, :, 1024 + gi * SSD_N:1024 + (gi + 1) * SSD_N].astype(bf16)
        cg = xbc[:, :, 1536 + gi * SSD_N:1536 + (gi + 1) * SSD_N].astype(bf16)
        gmat = jnp.einsum('bln,bmn->blm', cg, bg, preferred_element_type=f32)
        heads = [gi * n_in_grp + e for e in range(n_in_grp)]
        ws_l = []
        for hd in heads:
            ln = LANE_DT + hd
            seg = jnp.where(causal, cs[:, :, ln:ln + 1] - rows[:, ln:ln + 1, :], NEG)
            ws_l.append((gmat * jnp.exp(seg) * dt_rows[:, ln:ln + 1, :]).astype(bf16))
        if L % 128 == 0:
            gsl = slice(gi * SSD_GW, (gi + 1) * SSD_GW)
            xg = xbc[:, :, gsl]
            xgb = xg.astype(bf16)
            blk = lax.broadcasted_iota(jnp.int32, (1, 1, SSD_GW), 2) // SSD_P

            def per_head(cols):
                out = cols[n_in_grp - 1]
                for e in range(n_in_grp - 2, -1, -1):
                    out = jnp.where(blk == e, cols[e], out)
                return out

            xbd = jnp.concatenate([jnp.where(blk == e, xgb, jnp.zeros_like(xgb)) for e in range(n_in_grp)], axis=1)
            st = hs_src[:, heads[0]:heads[0] + n_in_grp].reshape(SB, SSD_GW, SSD_N)
            y = (jnp.einsum('blm,bmp->blp', jnp.concatenate(ws_l, axis=2), xbd, preferred_element_type=f32)
                 + jnp.einsum('bln,bpn->blp', cg, st.astype(bf16), preferred_element_type=f32)
                 * per_head([ecum[:, :, LANE_DT + hd:LANE_DT + hd + 1] for hd in heads]))
            ys_s[:, :, gsl] = y + sd_ref[:, gsl][None] * xg
            wsx = xg * per_head([wsd[:, :, LANE_DT + hd:LANE_DT + hd + 1] for hd in heads])
            upd = jnp.einsum('blp,bln->bpn', wsx.astype(bf16), bg, preferred_element_type=f32)
            for e, hd in enumerate(heads):
                ln = LANE_DT + hd
                hs_out[:, hd] = ecl[:, :, ln:ln + 1] * st[:, e * SSD_P:(e + 1) * SSD_P, :] + upd[:, e * SSD_P:(e + 1) * SSD_P, :]
        else:
            for e, hd in enumerate(heads):
                ln = LANE_DT + hd
                xh = xbc[:, :, hd * SSD_P:(hd + 1) * SSD_P]
                st = hs_src[:, hd]
                y = (jnp.einsum('blm,bmp->blp', ws_l[e], xh.astype(bf16), preferred_element_type=f32)
                     + jnp.einsum('bln,bpn->blp', cg, st.astype(bf16), preferred_element_type=f32) * ecum[:, :, ln:ln + 1])
                ys_s[:, :, hd * SSD_P:(hd + 1) * SSD_P] = y + sd_ref[:, hd * SSD_P:(hd + 1) * SSD_P][None] * xh
                wsx = xh * wsd[:, :, ln:ln + 1]
                hs_out[:, hd] = ecl[:, :, ln:ln + 1] * st + jnp.einsum('blp,bln->bpn', wsx.astype(bf16), bg, preferred_element_type=f32)
    ysz = ys_s[...] * jax.nn.silu(sz_ref[...])
    y_ref[:, :, 2 * D_MODEL:3 * D_MODEL] = (
        ysz * lax.rsqrt(jnp.mean(ysz * ysz, axis=2, keepdims=True) + EPS) * sg_ref[...][None]).astype(bf16)


def _full_spec(arr):
    nd = arr.ndim
    return pl.BlockSpec(arr.shape, lambda i, c, _nd=nd: (0,) * _nd)


def _mixer(u_main, u_small, states, lin, prev_out, lout, params, L, SB):
    B, T, _ = u_main.shape
    NC = T // L

    def state_spec(arr):
        rest = arr.shape[2:]
        nz = (0,) * len(rest)
        if arr.shape[1] == B:
            return pl.BlockSpec((None, SB) + rest, lambda i, c: (lin, i) + nz)
        return pl.BlockSpec((None, 1) + rest, lambda i, c: (lin, 0) + nz)

    in_specs = [pl.BlockSpec((SB, L, COLBLK), lambda i, c, _b=b: (i, c, _b)) for b in range(N_MIX // COLBLK)]
    in_specs.append(pl.BlockSpec((SB, L, N_SMALL), lambda i, c: (i, c, 0)))
    in_specs += [state_spec(s) for s in states]
    in_specs += [_full_spec(p) for p in params]
    n_alias = 0 if prev_out is None else N_STATE
    aliases = {}
    if prev_out is not None:
        first = len(in_specs)
        in_specs += [pl.BlockSpec(memory_space=pl.ANY)] * N_STATE
        aliases = {first + j: 1 + j for j in range(N_STATE)}
    out_state_shapes = [(DEPTH, B) + s.shape[2:] for s in states]
    out_specs = [pl.BlockSpec((SB, L, 3 * D_MODEL), lambda i, c: (i, c, 0))]
    out_specs += [pl.BlockSpec((None, SB) + shp[2:], lambda i, c, _n=len(shp) - 2: (lout, i) + (0,) * _n)
                  for shp in out_state_shapes]
    out_shape = [jax.ShapeDtypeStruct((B, T, 3 * D_MODEL), bf16)]
    out_shape += [jax.ShapeDtypeStruct(shp, f32) for shp in out_state_shapes]
    scratch = [
        pltpu.VMEM((LRU_W // 128, SB, SUBLANES + L, 128), f32),
        pltpu.VMEM((SSD_CONV_CH // 128, SB, SUBLANES + L, 128), f32),
        pltpu.VMEM((SB, L, LRU_W), f32),
        pltpu.VMEM((SB, L, LRU_W), f32),
        pltpu.VMEM((SB, L, LRU_W), f32),
        pltpu.VMEM((SB, L, D_MODEL), f32),
    ]
    extra = () if prev_out is None else tuple(prev_out)
    outs = pl.pallas_call(
        functools.partial(_mixer_kernel, L, SB, NC, n_alias),
        grid=(B // SB, NC),
        in_specs=in_specs,
        out_specs=out_specs,
        out_shape=out_shape,
        scratch_shapes=scratch,
        input_output_aliases=aliases,
        compiler_params=_cparams(("arbitrary", "arbitrary")),
        name="mixer",
    )(*([u_main] * (N_MIX // COLBLK)), u_small, *states, *params, *extra)
    return outs[0], tuple(outs[1:])


def _outproj_kernel(final, y_ref, gt_ref, x_ref, wml_ref, wlru_ref, wssd_ref, wout_ref, fg_ref, o_ref):
    gt = jax.nn.sigmoid(gt_ref[...])
    merged = gt[:, 0:D_MODEL] * jnp.dot(y_ref[:, 0:D_MODEL], wml_ref[...], preferred_element_type=f32)
    merged = merged + gt[:, D_MODEL:2 * D_MODEL] * jnp.dot(y_ref[:, D_MODEL:2 * D_MODEL], wlru_ref[...], preferred_element_type=f32)
    merged = merged + gt[:, 2 * D_MODEL:] * jnp.dot(y_ref[:, 2 * D_MODEL:], wssd_ref[...], preferred_element_type=f32)
    xn = x_ref[...] + jnp.dot(merged.astype(bf16), wout_ref[...], preferred_element_type=f32)
    if final:
        xn = xn * lax.rsqrt(jnp.mean(xn * xn, axis=-1, keepdims=True) + EPS) * fg_ref[...]
    o_ref[...] = xn


def _outproj(y2d, u_main2d, x2d, wml, wlru, wssd, wout, fg, final):
    t = x2d.shape[0]
    tm = min(t, 256)
    wspec = pl.BlockSpec((D_MODEL, D_MODEL), lambda i: (0, 0))
    return pl.pallas_call(
        functools.partial(_outproj_kernel, final),
        grid=(t // tm,),
        in_specs=[
            pl.BlockSpec((tm, 3 * D_MODEL), lambda i: (i, 0)),
            pl.BlockSpec((tm, N_GATE), lambda i: (i, N_MIX // N_GATE)),
            pl.BlockSpec((tm, D_MODEL), lambda i: (i, 0)),
            wspec, wspec, wspec, wspec,
            pl.BlockSpec((1, D_MODEL), lambda i: (0, 0)),
        ],
        out_specs=pl.BlockSpec((tm, D_MODEL), lambda i: (i, 0)),
        out_shape=jax.ShapeDtypeStruct((t, D_MODEL), f32),
        compiler_params=_cparams(("arbitrary",)),
        name="outproj",
    )(y2d, u_main2d, x2d, wml, wlru, wssd, wout, fg)


def _lane_pad(v, offset):
    n = v.shape[-1]
    pad = [(0, 0)] * (v.ndim - 1) + [(offset, N_SMALL - offset - n)]
    return jnp.pad(v, pad)


def kernel(x_prompt, x_sample, state_mlstm_c, state_mlstm_n, state_mlstm_m, state_rglru_h, state_rglru_conv, state_ssd_h, state_ssd_conv, meta_tokens, w_in, norm_g, ml_f_bias, ml_norm_g, lru_conv_w, lru_conv_b, lru_w_a, lru_b_a, lru_w_x, lru_b_x, lru_lambda, ssd_conv_w, ssd_conv_b, ssd_dt_bias, ssd_a_log, ssd_d, ssd_norm_g, w_br_ml, w_br_lru, w_br_ssd, w_out, final_norm_g):
    dseq = x_sample.shape[1]

    o_i = 2 * ML_HEADS * ML_DQK + ML_HEADS * ML_DV
    o_o = o_i + 2 * ML_HEADS
    o_dt = o_o + N_MIX - o_i
    o_g = o_dt + SSD_HEADS
    w_main = jnp.concatenate([w_in[:, :, :o_i], w_in[:, :, o_o:o_dt], w_in[:, :, o_g:]], axis=2).astype(bf16)
    w_small = jnp.concatenate(
        [w_in[:, :, o_i:o_o], w_in[:, :, o_dt:o_g],
         jnp.zeros((DEPTH, D_MODEL, N_SMALL - 2 * ML_HEADS - SSD_HEADS), w_in.dtype)], axis=2).astype(bf16)
    wa = lru_w_a.reshape(DEPTH, LRU_BLOCKS // 2, 2, LRU_BD, LRU_BD)
    wx = lru_w_x.reshape(DEPTH, LRU_BLOCKS // 2, 2, LRU_BD, LRU_BD)
    zb = jnp.zeros_like(wa[:, :, 0])

    def pair(w):
        top = jnp.concatenate([w[:, :, 0], zb], axis=-1)
        bot = jnp.concatenate([zb, w[:, :, 1]], axis=-1)
        return jnp.concatenate([top, bot], axis=-2)

    w_ax = jnp.concatenate([pair(wa), pair(wx)], axis=-1).astype(bf16)
    fb_pad = _lane_pad(ml_f_bias, LANE_F)[:, None, :]
    dtb_pad = _lane_pad(ssd_dt_bias, LANE_DT)[:, None, :]
    alog_pad = _lane_pad(ssd_a_log, LANE_DT)[:, None, :]
    sd_full = jnp.repeat(ssd_d, SSD_P, axis=1)[:, None, :]
    wml, wlru, wssd, wo = (w.astype(bf16) for w in (w_br_ml, w_br_lru, w_br_ssd, w_out))
    fg = final_norm_g[None, :]

    def layer_params(l):
        return (fb_pad[l], ml_norm_g[l].reshape(1, ML_HEADS * ML_DV), lru_conv_w[l], lru_conv_b[l][None],
                w_ax[l], lru_b_a[l][None], lru_b_x[l][None], lru_lambda[l][None],
                ssd_conv_w[l], ssd_conv_b[l][None], dtb_pad[l], alog_pad[l], sd_full[l], ssd_norm_g[l][None])

    zero_states = (jnp.zeros((1, 1, ML_HEADS, ML_DQK, ML_DV), f32), jnp.zeros((1, 1, ML_HEADS, ML_DQK), f32),
                   jnp.zeros((1, 1, 1, N_SMALL), f32), jnp.zeros((1, 1, 1, LRU_W), f32),
                   jnp.zeros((1, 1, CONV_W - 1, LRU_W), f32), jnp.zeros((1, 1, SSD_HEADS, SSD_P, SSD_N), f32),
                   jnp.zeros((1, 1, CONV_W - 1, SSD_CONV_CH), f32))
    sample_states = (state_mlstm_c, state_mlstm_n, _lane_pad(state_mlstm_m, LANE_F)[:, :, None, :],
                     state_rglru_h[:, :, None, :], state_rglru_conv, state_ssd_h, state_ssd_conv)

    def run_layer(l, x3, states, lin, prev_out, L, SB, need_x):
        b, t, _ = x3.shape
        x2 = x3.reshape(b * t, D_MODEL)
        u_main, u_small = _proj(x2, norm_g[l][None], w_main[l], w_small[l])
        y, new_states = _mixer(u_main.reshape(b, t, N_MAIN), u_small.reshape(b, t, N_SMALL),
                               states, lin, prev_out, l, layer_params(l), L, SB)
        if not need_x:
            return None, new_states
        x_new = _outproj(y.reshape(b * t, 3 * D_MODEL), u_main, x2, wml[l], wlru[l], wssd[l], wo[l], fg,
                         final=(l == DEPTH - 1))
        return x_new.reshape(b, t, D_MODEL), new_states

    def unpack(st):
        c, n, m, hl, cl, hs, cs = st
        return (c, n, m[:, :, 0, LANE_F:LANE_F + ML_HEADS], hl[:, :, 0, :], cl, hs, cs)

    x_meta = meta_tokens[None]
    x_main = x_prompt
    x_samp = x_sample
    st_meta = st_main = st_samp = None
    for l in range(DEPTH):
        last = l == DEPTH - 1
        x_meta, st_meta = run_layer(l, x_meta, zero_states, 0, st_meta, N_META, 1, need_x=not last)
        x_main, st_main = run_layer(l, x_main, st_meta, l, st_main, 128, 2, need_x=True)
        x_samp, st_samp = run_layer(l, x_samp, sample_states, l, st_samp, dseq, 8, need_x=True)

    return (x_main, x_samp) + unpack(st_main) + unpack(st_samp)
```

```python
import functools

import jax
import jax.numpy as jnp
from jax import lax
from jax.experimental import pallas as pl
from jax.experimental.pallas import tpu as pltpu

f32 = jnp.float32
bf16 = jnp.bfloat16

D_MODEL = 1024
DEPTH = 2
N_META = 16
EPS = 1e-6
CONV_W = 4
ML_HEADS = 4
ML_DQK = 128
ML_DV = 256
ML_SCALE = ML_DQK ** -0.5
LRU_W = 1024
LRU_BLOCKS = 16
LRU_BD = 64
LRU_C = 8.0
SSD_HEADS = 16
SSD_P = 64
SSD_GROUPS = 4
SSD_N = 128
SSD_CONV_CH = 2048
SSD_GW = (SSD_HEADS // SSD_GROUPS) * SSD_P

COLBLK = 1024
BLK_QK, BLK_V, BLK_O, BLK_Z, BLK_LX, BLK_LZ, BLK_SX, BLK_SBC, BLK_SZ = range(9)
MIX_BLOCKS = (BLK_QK, BLK_V, BLK_O, BLK_Z, BLK_LX, BLK_LZ, BLK_SZ, BLK_SX, BLK_SBC)
N_MIX = 9 * COLBLK
N_GATE = 3 * D_MODEL
N_MAIN = N_MIX + N_GATE
WTILE = 2 * COLBLK
N_WTILES = N_MAIN // WTILE
N_SMALL = 128
LANE_I, LANE_F, LANE_DT = 0, 4, 8
SUBLANES = 8

NEG = -1e30
VMEM_LIMIT = 56 * 1024 * 1024
N_STATE = 7
N_PARAM = 14


def _cparams(sem):
    return pltpu.CompilerParams(dimension_semantics=sem, vmem_limit_bytes=VMEM_LIMIT)


def _rmsnorm(x, g):
    return x * lax.rsqrt(jnp.mean(x * x, axis=-1, keepdims=True) + EPS) * g


def _proj_kernel(x_ref, g_ref, w_ref, ws_ref, u_ref, us_ref, xn_ref):
    @pl.when(pl.program_id(1) == 0)
    def _():
        xn_ref[...] = _rmsnorm(x_ref[...], g_ref[...]).astype(bf16)
        us_ref[...] = jnp.dot(xn_ref[...], ws_ref[...], preferred_element_type=f32)

    u_ref[...] = jnp.dot(xn_ref[...], w_ref[...], preferred_element_type=f32)


def _proj(x2d, g, w_tiles, w_small):
    t = x2d.shape[0]
    tm = min(t, 1024)
    return pl.pallas_call(
        _proj_kernel,
        grid=(t // tm, N_WTILES),
        in_specs=[
            pl.BlockSpec((tm, D_MODEL), lambda i, j: (i, 0)),
            pl.BlockSpec((1, D_MODEL), lambda i, j: (0, 0)),
            pl.BlockSpec((None, D_MODEL, WTILE), lambda i, j: (j, 0, 0)),
            pl.BlockSpec((D_MODEL, N_SMALL), lambda i, j: (0, 0)),
        ],
        out_specs=[
            pl.BlockSpec((tm, WTILE), lambda i, j: (i, j)),
            pl.BlockSpec((tm, N_SMALL), lambda i, j: (i, 0)),
        ],
        out_shape=[
            jax.ShapeDtypeStruct((t, N_MAIN), f32),
            jax.ShapeDtypeStruct((t, N_SMALL), f32),
        ],
        scratch_shapes=[pltpu.VMEM((tm, D_MODEL), bf16)],
        compiler_params=_cparams(("arbitrary", "arbitrary")),
        name="proj",
    )(x2d, g, w_tiles, w_small)


def _cumsum_rows(x3, seg_len):
    sb = x3.shape[0]
    x = x3.reshape(sb * seg_len, x3.shape[2])
    t = lax.broadcasted_iota(jnp.int32, x.shape, 0) & (seg_len - 1)
    s = 1
    while s < seg_len:
        x = x + jnp.where(t >= s, pltpu.roll(x, s, 0), 0.0)
        s *= 2
    return x.reshape(x3.shape)


def _one_minus_sq(log_a, a):
    return jnp.tanh(-log_a) * (1.0 + a * a)


def _lane_masks():
    lane = lax.broadcasted_iota(jnp.int32, (1, 1, N_SMALL), 2)
    is_f = jnp.abs(2 * lane - (2 * LANE_F + ML_HEADS - 1)) < ML_HEADS
    is_dt = jnp.abs(2 * lane - (2 * LANE_DT + SSD_HEADS - 1)) < SSD_HEADS
    return is_f, is_dt


def _causal_mask(L):
    ri = lax.broadcasted_iota(jnp.int32, (1, L, L), 1)
    ci = lax.broadcasted_iota(jnp.int32, (1, L, L), 2)
    return ri >= ci


def _head_scalars(sm, fb_ref, dtb_ref, alog_ref, L):
    sb = sm.shape[0]
    is_f, is_dt = _lane_masks()
    logf = jax.nn.log_sigmoid(sm + fb_ref[...][None])
    dt = jax.nn.softplus(sm + dtb_ref[...][None])
    a_neg = -jnp.exp(alog_ref[...])[None]
    cs = _cumsum_rows(jnp.where(is_f, logf, jnp.where(is_dt, dt * a_neg, 0.0)), L)
    li4 = pltpu.roll(sm.reshape(sb * L, N_SMALL), LANE_F - LANE_I, 1).reshape(sb, L, N_SMALL)
    return cs, dt, li4


def _row_form(cs, li4):
    is_f, _ = _lane_masks()
    return jnp.swapaxes(jnp.where(is_f, cs - li4, cs), 1, 2)


def _mlstm_core(L, q_of, k_of, v_of, cs, li4, c_src, n_src, m_src, c_out, n_out, m_out, hout_put):
    is_f, _ = _lane_masks()
    causal = _causal_mask(L)
    m0 = m_src[...]
    bl = cs[:, L - 1:L, :]
    inter = cs + m0
    g = bl - cs + li4
    m_new = jnp.maximum(bl + m0, jnp.max(g, axis=1, keepdims=True))
    ws = jnp.exp(g - m_new) * ML_SCALE
    wc = jnp.exp(bl + m0 - m_new)
    rows = _row_form(cs, li4)
    for h in range(ML_HEADS):
        lf = LANE_F + h
        q = q_of(h)
        k = k_of(h)
        vb = v_of(h).astype(bf16)
        qb = q.astype(bf16)
        kb = k.astype(bf16)
        d = jnp.where(causal, cs[:, :, lf:lf + 1] - rows[:, lf:lf + 1, :], NEG)
        icol = inter[:, :, lf:lf + 1]
        mt = jnp.maximum(icol, jnp.max(d, axis=2, keepdims=True))
        s = jnp.einsum('bld,bmd->blm', qb, kb, preferred_element_type=f32) * (jnp.exp(d - mt) * ML_SCALE)
        w_int = jnp.exp(icol - mt)
        c_h = c_src[:, h]
        n_h = n_src[:, h:h + 1, :]
        num = (jnp.einsum('blm,bmv->blv', s.astype(bf16), vb, preferred_element_type=f32)
               + w_int * jnp.einsum('bld,bdv->blv', qb, c_h.astype(bf16), preferred_element_type=f32))
        den = jnp.sum(s, axis=2, keepdims=True) + w_int * jnp.sum(q.astype(f32) * n_h, axis=2, keepdims=True)
        hout_put(h, num / jnp.maximum(jnp.abs(den), jnp.exp(-mt)))
        kw = k.astype(f32) * ws[:, :, lf:lf + 1]
        wc_h = wc[:, :, lf:lf + 1]
        c_out[:, h] = wc_h * c_h + jnp.einsum('bld,blv->bdv', kw.astype(bf16), vb, preferred_element_type=f32)
        n_out[:, h:h + 1, :] = wc_h * n_h + jnp.sum(kw, axis=1, keepdims=True)
    m_out[...] = jnp.where(is_f, m_new, 0.0)


def _mlstm_gate(hout, o, z, g_row):
    hn = hout * lax.rsqrt(jnp.mean(hout * hout, axis=2, keepdims=True) + EPS) * g_row[None]
    return (hn * (jax.nn.sigmoid(o) * jax.nn.silu(z))).astype(bf16)


def _causal_conv(pad_ref, carry_ref, x_parts, w_ref, b_ref, L):
    lo = SUBLANES - (CONV_W - 1)
    col = 0
    for xp in x_parts:
        for s in range(xp.shape[2] // 128):
            pad_ref[col + s, :, SUBLANES:SUBLANES + L, :] = xp[:, :, s * 128:(s + 1) * 128]
        col += xp.shape[2] // 128
    outs, carry = [], []
    for c in range(col):
        sl = slice(c * 128, (c + 1) * 128)
        pad_ref[c, :, lo:SUBLANES, :] = carry_ref[:, :, sl]
        y = b_ref[:, sl][None]
        for j in range(CONV_W):
            y = y + pad_ref[c, :, lo + j:lo + j + L, :] * w_ref[j:j + 1, sl][None]
        outs.append(y)
        carry.append(pad_ref[c, :, lo + L:SUBLANES + L, :])
    return jnp.concatenate(outs, axis=2), jnp.concatenate(carry, axis=2)


def _lru_branch(L, lx, lz, pad_ref, cl_src, cl_out, hl_src, hl_out, lcw_ref, lcb_ref, wax_ref, ba_ref, bx_ref,
                lam_ref, a_s, u_s):
    sb = lx.shape[0]
    xc, cl_new = _causal_conv(pad_ref, cl_src, (lx,), lcw_ref, lcb_ref, L)
    cl_out[...] = cl_new
    xc2 = xc.reshape(sb * L, LRU_W)
    for p in range(LRU_BLOCKS // 2):
        sl = slice(p * 128, (p + 1) * 128)
        xp = xc2[:, sl]
        ra = jnp.dot(xp.astype(bf16), wax_ref[p], preferred_element_type=f32)
        r = jax.nn.sigmoid(ra[:, :128] + ba_ref[:, sl])
        i = jax.nn.sigmoid(ra[:, 128:] + bx_ref[:, sl])
        log_a = -LRU_C * r * jax.nn.softplus(-lam_ref[:, sl])
        a = jnp.exp(log_a)
        a_s[:, :, sl] = a.reshape(sb, L, 128)
        u_s[:, :, sl] = (jnp.sqrt(_one_minus_sq(log_a, a)) * (i * xp)).reshape(sb, L, 128)

    def lru_step(t, hcur):
        hcur = a_s[:, pl.ds(t, 1), :] * hcur + u_s[:, pl.ds(t, 1), :]
        a_s[:, pl.ds(t, 1), :] = hcur
        return hcur

    hl_out[...] = lax.fori_loop(0, L, lru_step, hl_src[...])
    return (a_s[...] * jax.nn.silu(lz)).astype(bf16)


def _ssd_core(L, sx, sbc, pad_ref, cs_src, cs_out, hs_src, hs_out, cs, dt, li4, scw_ref, scb_ref, sd_ref, ys_put):
    sb = sx.shape[0]
    causal = _causal_mask(L)
    rows = _row_form(cs, li4)
    dt_rows = jnp.swapaxes(dt, 1, 2)
    xbc, cs_new = _causal_conv(pad_ref, cs_src, (sx, sbc), scw_ref, scb_ref, L)
    cs_out[...] = cs_new
    xbc = jax.nn.silu(xbc)
    cl = cs[:, L - 1:L, :]
    wsd = jnp.exp(cl - cs) * dt
    ecum = jnp.exp(cs)
    ecl = jnp.exp(cl)
    n_in_grp = SSD_HEADS // SSD_GROUPS
    for gi in range(SSD_GROUPS):
        bg = xbc[:, :, 1024 + gi * SSD_N:1024 + (gi + 1) * SSD_N].astype(bf16)
        cg = xbc[:, :, 1536 + gi * SSD_N:1536 + (gi + 1) * SSD_N].astype(bf16)
        gmat = jnp.einsum('bln,bmn->blm', cg, bg, preferred_element_type=f32)
        heads = [gi * n_in_grp + e for e in range(n_in_grp)]
        ws_l = []
        for hd in heads:
            ln = LANE_DT + hd
            seg = jnp.where(causal, cs[:, :, ln:ln + 1] - rows[:, ln:ln + 1, :], NEG)
            ws_l.append((gmat * jnp.exp(seg) * dt_rows[:, ln:ln + 1, :]).astype(bf16))
        if L % 128 == 0:
            gsl = slice(gi * SSD_GW, (gi + 1) * SSD_GW)
            xg = xbc[:, :, gsl]
            xgb = xg.astype(bf16)
            blk = lax.broadcasted_iota(jnp.int32, (1, 1, SSD_GW), 2) // SSD_P

            def per_head(cols):
                out = cols[n_in_grp - 1]
                for e in range(n_in_grp - 2, -1, -1):
                    out = jnp.where(blk == e, cols[e], out)
                return out

            xbd = jnp.concatenate([jnp.where(blk == e, xgb, jnp.zeros_like(xgb)) for e in range(n_in_grp)], axis=1)
            st = hs_src[:, heads[0]:heads[0] + n_in_grp].reshape(sb, SSD_GW, SSD_N)
            y = (jnp.einsum('blm,bmp->blp', jnp.concatenate(ws_l, axis=2), xbd, preferred_element_type=f32)
                 + jnp.einsum('bln,bpn->blp', cg, st.astype(bf16), preferred_element_type=f32)
                 * per_head([ecum[:, :, LANE_DT + hd:LANE_DT + hd + 1] for hd in heads]))
            ys_put(gsl, y + sd_ref[:, gsl][None] * xg)
            wsx = xg * per_head([wsd[:, :, LANE_DT + hd:LANE_DT + hd + 1] for hd in heads])
            upd = jnp.einsum('blp,bln->bpn', wsx.astype(bf16), bg, preferred_element_type=f32)
            for e, hd in enumerate(heads):
                ln = LANE_DT + hd
                hs_out[:, hd] = ecl[:, :, ln:ln + 1] * st[:, e * SSD_P:(e + 1) * SSD_P, :] + upd[:, e * SSD_P:(e + 1) * SSD_P, :]
        else:
            for e, hd in enumerate(heads):
                ln = LANE_DT + hd
                hsl = slice(hd * SSD_P, (hd + 1) * SSD_P)
                xh = xbc[:, :, hsl]
                st = hs_src[:, hd]
                y = (jnp.einsum('blm,bmp->blp', ws_l[e], xh.astype(bf16), preferred_element_type=f32)
                     + jnp.einsum('bln,bpn->blp', cg, st.astype(bf16), preferred_element_type=f32) * ecum[:, :, ln:ln + 1])
                ys_put(hsl, y + sd_ref[:, hsl][None] * xh)
                wsx = xh * wsd[:, :, ln:ln + 1]
                hs_out[:, hd] = ecl[:, :, ln:ln + 1] * st + jnp.einsum('blp,bln->bpn', wsx.astype(bf16), bg, preferred_element_type=f32)


def _ssd_gate(ys, sz, sg_ref):
    ysz = ys * jax.nn.silu(sz)
    return (ysz * lax.rsqrt(jnp.mean(ysz * ysz, axis=2, keepdims=True) + EPS) * sg_ref[...][None]).astype(bf16)


def _carry_state(first_chunk, ins, outs):
    @pl.when(first_chunk)
    def _():
        for src, dst in zip(ins, outs):
            dst[...] = jnp.broadcast_to(src[...], dst.shape)


def _mixer_kernel(L, SB, NC, n_alias, *refs):
    (qk_ref, v_ref, o_ref, z_ref, lx_ref, lz_ref, sz_ref, sx_ref, sbc_ref, sm_ref) = refs[:10]
    st_in = refs[10:10 + N_STATE]
    (fb_ref, mg_ref, lcw_ref, lcb_ref, wax_ref, ba_ref, bx_ref, lam_ref,
     scw_ref, scb_ref, dtb_ref, alog_ref, sd_ref, sg_ref) = refs[10 + N_STATE:10 + N_STATE + N_PARAM]
    rest = refs[10 + N_STATE + N_PARAM + n_alias:]
    y_ref = rest[0]
    st_out = rest[1:1 + N_STATE]
    pad, a_s, u_s, ys_s = rest[1 + N_STATE:]

    if NC == 1:
        st_src = st_in
    else:
        st_src = st_out
        _carry_state(pl.program_id(1) == 0, st_in, st_out)
    c_src, n_src, m_src, hl_src, cl_src, hs_src, cs_src = st_src
    c_out, n_out, m_out, hl_out, cl_out, hs_out, cs_out = st_out

    cs, dt, li4 = _head_scalars(sm_ref[...], fb_ref, dtb_ref, alog_ref, L)

    def hout_put(h, hout):
        sl = slice(h * ML_DV, (h + 1) * ML_DV)
        y_ref[:, :, sl] = _mlstm_gate(hout, o_ref[:, :, sl], z_ref[:, :, sl], mg_ref[:, sl])

    _mlstm_core(L,
                lambda h: qk_ref[:, :, h * ML_DQK:(h + 1) * ML_DQK],
                lambda h: qk_ref[:, :, (ML_HEADS + h) * ML_DQK:(ML_HEADS + h + 1) * ML_DQK],
                lambda h: v_ref[:, :, h * ML_DV:(h + 1) * ML_DV],
                cs, li4, c_src, n_src, m_src, c_out, n_out, m_out, hout_put)

    y_ref[:, :, D_MODEL:2 * D_MODEL] = _lru_branch(
        L, lx_ref[...], lz_ref[...], pad, cl_src, cl_out, hl_src, hl_out,
        lcw_ref, lcb_ref, wax_ref, ba_ref, bx_ref, lam_ref, a_s, u_s)

    def ys_put(sl, val):
        ys_s[:, :, sl] = val

    _ssd_core(L, sx_ref[...], sbc_ref[...], pad, cs_src, cs_out, hs_src, hs_out, cs, dt, li4,
              scw_ref, scb_ref, sd_ref, ys_put)
    y_ref[:, :, 2 * D_MODEL:3 * D_MODEL] = _ssd_gate(ys_s[...], sz_ref[...], sg_ref)


def _full_spec(arr, ngrid):
    nd = arr.ndim
    return pl.BlockSpec(arr.shape, lambda *g, _nd=nd: (0,) * _nd)


def _state_specs(states, lin, lout, B, SB):
    def in_spec(arr):
        rest = arr.shape[2:]
        nz = (0,) * len(rest)
        if arr.shape[1] == B:
            return pl.BlockSpec((None, SB) + rest, lambda i, c: (lin, i) + nz)
        return pl.BlockSpec((None, 1) + rest, lambda i, c: (lin, 0) + nz)

    out_shapes = [(DEPTH, B) + s.shape[2:] for s in states]
    out_specs = [pl.BlockSpec((None, SB) + shp[2:], lambda i, c, _n=len(shp) - 2: (lout, i) + (0,) * _n)
                 for shp in out_shapes]
    return [in_spec(s) for s in states], out_specs, [jax.ShapeDtypeStruct(shp, f32) for shp in out_shapes]


def _mixer(u_main, u_small, states, lin, prev_out, lout, params, L, SB):
    B, T, _ = u_main.shape
    NC = T // L
    st_in_specs, st_out_specs, st_out_shape = _state_specs(states, lin, lout, B, SB)
    in_specs = [pl.BlockSpec((SB, L, COLBLK), lambda i, c, _b=b: (i, c, _b)) for b in MIX_BLOCKS]
    in_specs.append(pl.BlockSpec((SB, L, N_SMALL), lambda i, c: (i, c, 0)))
    in_specs += st_in_specs
    in_specs += [_full_spec(p, 2) for p in params]
    n_alias = 0 if prev_out is None else N_STATE
    aliases = {}
    if prev_out is not None:
        first = len(in_specs)
        in_specs += [pl.BlockSpec(memory_space=pl.ANY)] * N_STATE
        aliases = {first + j: 1 + j for j in range(N_STATE)}
    out_specs = [pl.BlockSpec((SB, L, 3 * D_MODEL), lambda i, c: (i, c, 0))] + st_out_specs
    out_shape = [jax.ShapeDtypeStruct((B, T, 3 * D_MODEL), bf16)] + st_out_shape
    scratch = [
        pltpu.VMEM((SSD_CONV_CH // 128, SB, SUBLANES + L, 128), f32),
        pltpu.VMEM((SB, L, LRU_W), f32),
        pltpu.VMEM((SB, L, LRU_W), f32),
        pltpu.VMEM((SB, L, D_MODEL), f32),
    ]
    extra = () if prev_out is None else tuple(prev_out)
    outs = pl.pallas_call(
        functools.partial(_mixer_kernel, L, SB, NC, n_alias),
        grid=(B // SB, NC),
        in_specs=in_specs,
        out_specs=out_specs,
        out_shape=out_shape,
        scratch_shapes=scratch,
        input_output_aliases=aliases,
        compiler_params=_cparams(("arbitrary", "arbitrary")),
        name="mixer",
    )(*([u_main] * len(MIX_BLOCKS)), u_small, *states, *params, *extra)
    return outs[0], tuple(outs[1:])


def _outproj_kernel(final, y_ref, gt_ref, x_ref, wml_ref, wlru_ref, wssd_ref, wout_ref, fg_ref, o_ref):
    gt = jax.nn.sigmoid(gt_ref[...])
    merged = gt[:, 0:D_MODEL] * jnp.dot(y_ref[:, 0:D_MODEL], wml_ref[...], preferred_element_type=f32)
    merged = merged + gt[:, D_MODEL:2 * D_MODEL] * jnp.dot(y_ref[:, D_MODEL:2 * D_MODEL], wlru_ref[...], preferred_element_type=f32)
    merged = merged + gt[:, 2 * D_MODEL:] * jnp.dot(y_ref[:, 2 * D_MODEL:], wssd_ref[...], preferred_element_type=f32)
    xn = x_ref[...] + jnp.dot(merged.astype(bf16), wout_ref[...], preferred_element_type=f32)
    if final:
        xn = _rmsnorm(xn, fg_ref[...])
    o_ref[...] = xn


def _outproj(y2d, u_main2d, x2d, wml, wlru, wssd, wout, fg, final):
    t = x2d.shape[0]
    tm = min(t, 256)
    wspec = pl.BlockSpec((D_MODEL, D_MODEL), lambda i: (0, 0))
    return pl.pallas_call(
        functools.partial(_outproj_kernel, final),
        grid=(t // tm,),
        in_specs=[
            pl.BlockSpec((tm, 3 * D_MODEL), lambda i: (i, 0)),
            pl.BlockSpec((tm, N_GATE), lambda i: (i, N_MIX // N_GATE)),
            pl.BlockSpec((tm, D_MODEL), lambda i: (i, 0)),
            wspec, wspec, wspec, wspec,
            pl.BlockSpec((1, D_MODEL), lambda i: (0, 0)),
        ],
        out_specs=pl.BlockSpec((tm, D_MODEL), lambda i: (i, 0)),
        out_shape=jax.ShapeDtypeStruct((t, D_MODEL), f32),
        compiler_params=_cparams(("arbitrary",)),
        name="outproj",
    )(y2d, u_main2d, x2d, wml, wlru, wssd, wout, fg)


def _layer_kernel(L, SB, final, n_alias, *refs):
    x_ref, w_hbm, wsm_ref, ng_ref = refs[:4]
    st_in = refs[4:4 + N_STATE]
    (fb_ref, mg_ref, lcw_ref, lcb_ref, wax_ref, ba_ref, bx_ref, lam_ref,
     scw_ref, scb_ref, dtb_ref, alog_ref, sd_ref, sg_ref) = refs[4 + N_STATE:4 + N_STATE + N_PARAM]
    wml_ref, wlru_ref, wssd_ref, wout_ref, fg_ref = refs[4 + N_STATE + N_PARAM:9 + N_STATE + N_PARAM]
    rest = refs[9 + N_STATE + N_PARAM + n_alias:]
    xo_ref = rest[0]
    st_out = rest[1:1 + N_STATE]
    wbuf, wsem, xn_s, t0_s, sm_s, ua_s, ub_s, y_s, wk, pad = rest[1 + N_STATE:]

    i, c = pl.program_id(0), pl.program_id(1)
    first_step = jnp.logical_and(i == 0, c == 0)
    last_step = jnp.logical_and(i == pl.num_programs(0) - 1, c == pl.num_programs(1) - 1)
    R = SB * L

    def w_copy(tile):
        slot = tile % 2
        return pltpu.make_async_copy(w_hbm.at[tile], wbuf.at[slot], wsem.at[slot])

    def project(tile):
        return jnp.dot(xn_s[...], wbuf[tile % 2], preferred_element_type=f32).reshape(SB, L, WTILE)

    _carry_state(c == 0, st_in, st_out)
    c_st, n_st, m_st, hl_st, cl_st, hs_st, cs_st = st_out

    @pl.when(first_step)
    def _():
        w_copy(0).start()

    w_copy(1).start()
    xn_s[...] = _rmsnorm(x_ref[...].reshape(R, D_MODEL), ng_ref[...]).astype(bf16)
    sm_s[...] = jnp.dot(xn_s[...], wsm_ref[...], preferred_element_type=f32).reshape(SB, L, N_SMALL)
    w_copy(0).wait()
    t0_s[...] = project(0).astype(bf16)

    w_copy(2).start()
    w_copy(1).wait()
    ua_s[...] = project(1)
    cs, dt, li4 = _head_scalars(sm_s[...], fb_ref, dtb_ref, alog_ref, L)

    def hout_put(h, hout):
        wk[0, :, :, h * ML_DV:(h + 1) * ML_DV] = hout

    _mlstm_core(L,
                lambda h: t0_s[:, :, h * ML_DQK:(h + 1) * ML_DQK],
                lambda h: t0_s[:, :, (ML_HEADS + h) * ML_DQK:(ML_HEADS + h + 1) * ML_DQK],
                lambda h: t0_s[:, :, COLBLK + h * ML_DV:COLBLK + (h + 1) * ML_DV],
                cs, li4, c_st, n_st, m_st, c_st, n_st, m_st, hout_put)

    w_copy(3).start()
    w_copy(2).wait()
    ub_s[...] = project(2)
    for h in range(ML_HEADS):
        sl = slice(h * ML_DV, (h + 1) * ML_DV)
        y_s[:, :, sl] = _mlstm_gate(wk[0, :, :, sl], ua_s[:, :, sl],
                                    ua_s[:, :, COLBLK + h * ML_DV:COLBLK + (h + 1) * ML_DV], mg_ref[:, sl])

    w_copy(4).start()
    w_copy(3).wait()
    ua_s[...] = project(3)
    y_s[:, :, D_MODEL:2 * D_MODEL] = _lru_branch(
        L, ub_s[:, :, 0:COLBLK], ub_s[:, :, COLBLK:WTILE], pad, cl_st, cl_st, hl_st, hl_st,
        lcw_ref, lcb_ref, wax_ref, ba_ref, bx_ref, lam_ref, wk.at[1], wk.at[2])

    w_copy(5).start()
    w_copy(4).wait()
    ub_s[...] = project(4)

    def ys_put(sl, val):
        wk[1, :, :, sl] = val

    _ssd_core(L, ua_s[:, :, 0:COLBLK], ua_s[:, :, COLBLK:WTILE], pad, cs_st, cs_st, hs_st, hs_st, cs, dt, li4,
              scw_ref, scb_ref, sd_ref, ys_put)

    @pl.when(jnp.logical_not(last_step))
    def _():
        w_copy(0).start()

    w_copy(5).wait()
    ua_s[...] = project(5)
    y_s[:, :, 2 * D_MODEL:3 * D_MODEL] = _ssd_gate(wk[1], ub_s[:, :, 0:COLBLK], sg_ref)

    def branch(k, gate):
        yk = y_s[:, :, k * D_MODEL:(k + 1) * D_MODEL].reshape(R, D_MODEL)
        w_ref = (wml_ref, wlru_ref, wssd_ref)[k]
        return jax.nn.sigmoid(gate).reshape(R, D_MODEL) * jnp.dot(yk, w_ref[...], preferred_element_type=f32)

    merged = branch(0, ub_s[:, :, COLBLK:WTILE])

    merged = merged + branch(1, ua_s[:, :, 0:COLBLK])
    merged = merged + branch(2, ua_s[:, :, COLBLK:WTILE])
    xn = x_ref[...].reshape(R, D_MODEL) + jnp.dot(merged.astype(bf16), wout_ref[...], preferred_element_type=f32)
    if final:
        xn = _rmsnorm(xn, fg_ref[...])
    xo_ref[...] = xn.reshape(SB, L, D_MODEL)


def _layer(x3, w_tiles, w_small, norm_g, states, lin, prev_out, lout, params, out_w, final, L, SB):
    B, T, _ = x3.shape
    st_in_specs, st_out_specs, st_out_shape = _state_specs(states, lin, lout, B, SB)
    in_specs = [
        pl.BlockSpec((SB, L, D_MODEL), lambda i, c: (i, c, 0)),
        pl.BlockSpec(memory_space=pl.ANY),
        _full_spec(w_small, 2),
        _full_spec(norm_g, 2),
    ]
    in_specs += st_in_specs
    in_specs += [_full_spec(p, 2) for p in params]
    in_specs += [_full_spec(w, 2) for w in out_w]
    n_alias = 0 if prev_out is None else N_STATE
    aliases = {}
    if prev_out is not None:
        first = len(in_specs)
        in_specs += [pl.BlockSpec(memory_space=pl.ANY)] * N_STATE
        aliases = {first + j: 1 + j for j in range(N_STATE)}
    out_specs = [pl.BlockSpec((SB, L, D_MODEL), lambda i, c: (i, c, 0))] + st_out_specs
    out_shape = [jax.ShapeDtypeStruct((B, T, D_MODEL), f32)] + st_out_shape
    scratch = [
        pltpu.VMEM((2, D_MODEL, WTILE), bf16),
        pltpu.SemaphoreType.DMA((2,)),
        pltpu.VMEM((SB * L, D_MODEL), bf16),
        pltpu.VMEM((SB, L, WTILE), bf16),
        pltpu.VMEM((SB, L, N_SMALL), f32),
        pltpu.VMEM((SB, L, WTILE), f32),
        pltpu.VMEM((SB, L, WTILE), f32),
        pltpu.VMEM((SB, L, 3 * D_MODEL), bf16),
        pltpu.VMEM((3, SB, L, D_MODEL), f32),
        pltpu.VMEM((SSD_CONV_CH // 128, SB, SUBLANES + L, 128), f32),
    ]
    extra = () if prev_out is None else tuple(prev_out)
    outs = pl.pallas_call(
        functools.partial(_layer_kernel, L, SB, final, n_alias),
        grid=(B // SB, T // L),
        in_specs=in_specs,
        out_specs=out_specs,
        out_shape=out_shape,
        scratch_shapes=scratch,
        input_output_aliases=aliases,
        compiler_params=_cparams(("arbitrary", "arbitrary")),
        name="layer",
    )(x3, w_tiles, w_small, norm_g, *states, *params, *out_w, *extra)
    return outs[0], tuple(outs[1:])


def _lane_pad(v, offset):
    n = v.shape[-1]
    pad = [(0, 0)] * (v.ndim - 1) + [(offset, N_SMALL - offset - n)]
    return jnp.pad(v, pad)


def kernel(x_prompt, x_sample, state_mlstm_c, state_mlstm_n, state_mlstm_m, state_rglru_h, state_rglru_conv, state_ssd_h, state_ssd_conv, meta_tokens, w_in, norm_g, ml_f_bias, ml_norm_g, lru_conv_w, lru_conv_b, lru_w_a, lru_b_a, lru_w_x, lru_b_x, lru_lambda, ssd_conv_w, ssd_conv_b, ssd_dt_bias, ssd_a_log, ssd_d, ssd_norm_g, w_br_ml, w_br_lru, w_br_ssd, w_out, final_norm_g):
    dseq = x_sample.shape[1]

    o_i = 2 * ML_HEADS * ML_DQK + ML_HEADS * ML_DV
    o_o = o_i + 2 * ML_HEADS
    o_sz = o_o + 4 * COLBLK
    o_xbc = o_sz + COLBLK
    o_dt = o_xbc + SSD_CONV_CH
    o_g = o_dt + SSD_HEADS
    w_main = jnp.concatenate([w_in[:, :, :o_i], w_in[:, :, o_o:o_sz], w_in[:, :, o_xbc:o_dt],
                              w_in[:, :, o_sz:o_xbc], w_in[:, :, o_g:]], axis=2).astype(bf16)
    w_tiles = w_main.reshape(DEPTH, D_MODEL, N_WTILES, WTILE).transpose(0, 2, 1, 3)
    w_small = jnp.concatenate(
        [w_in[:, :, o_i:o_o], w_in[:, :, o_dt:o_g],
         jnp.zeros((DEPTH, D_MODEL, N_SMALL - 2 * ML_HEADS - SSD_HEADS), w_in.dtype)], axis=2).astype(bf16)
    wa = lru_w_a.reshape(DEPTH, LRU_BLOCKS // 2, 2, LRU_BD, LRU_BD)
    wx = lru_w_x.reshape(DEPTH, LRU_BLOCKS // 2, 2, LRU_BD, LRU_BD)
    zb = jnp.zeros_like(wa[:, :, 0])

    def pair(w):
        top = jnp.concatenate([w[:, :, 0], zb], axis=-1)
        bot = jnp.concatenate([zb, w[:, :, 1]], axis=-1)
        return jnp.concatenate([top, bot], axis=-2)

    w_ax = jnp.concatenate([pair(wa), pair(wx)], axis=-1).astype(bf16)
    fb_pad = _lane_pad(ml_f_bias, LANE_F)[:, None, :]
    dtb_pad = _lane_pad(ssd_dt_bias, LANE_DT)[:, None, :]
    alog_pad = _lane_pad(ssd_a_log, LANE_DT)[:, None, :]
    sd_full = jnp.repeat(ssd_d, SSD_P, axis=1)[:, None, :]
    wml, wlru, wssd, wo = (w.astype(bf16) for w in (w_br_ml, w_br_lru, w_br_ssd, w_out))
    fg = final_norm_g[None, :]

    def layer_params(l):
        return (fb_pad[l], ml_norm_g[l].reshape(1, ML_HEADS * ML_DV), lru_conv_w[l], lru_conv_b[l][None],
                w_ax[l], lru_b_a[l][None], lru_b_x[l][None], lru_lambda[l][None],
                ssd_conv_w[l], ssd_conv_b[l][None], dtb_pad[l], alog_pad[l], sd_full[l], ssd_norm_g[l][None])

    zero_states = (jnp.zeros((1, 1, ML_HEADS, ML_DQK, ML_DV), f32), jnp.zeros((1, 1, ML_HEADS, ML_DQK), f32),
                   jnp.zeros((1, 1, 1, N_SMALL), f32), jnp.zeros((1, 1, 1, LRU_W), f32),
                   jnp.zeros((1, 1, CONV_W - 1, LRU_W), f32), jnp.zeros((1, 1, SSD_HEADS, SSD_P, SSD_N), f32),
                   jnp.zeros((1, 1, CONV_W - 1, SSD_CONV_CH), f32))
    sample_states = (state_mlstm_c, state_mlstm_n, _lane_pad(state_mlstm_m, LANE_F)[:, :, None, :],
                     state_rglru_h[:, :, None, :], state_rglru_conv, state_ssd_h, state_ssd_conv)

    def run_short(l, x3, states, lin, prev_out, L, SB, need_x):
        b, t, _ = x3.shape
        x2 = x3.reshape(b * t, D_MODEL)
        u_main, u_small = _proj(x2, norm_g[l][None], w_tiles[l], w_small[l])
        y, new_states = _mixer(u_main.reshape(b, t, N_MAIN), u_small.reshape(b, t, N_SMALL),
                               states, lin, prev_out, l, layer_params(l), L, SB)
        if not need_x:
            return None, new_states
        x_new = _outproj(y.reshape(b * t, 3 * D_MODEL), u_main, x2, wml[l], wlru[l], wssd[l], wo[l], fg,
                         final=(l == DEPTH - 1))
        return x_new.reshape(b, t, D_MODEL), new_states

    def unpack(st):
        c, n, m, hl, cl, hs, cs = st
        return (c, n, m[:, :, 0, LANE_F:LANE_F + ML_HEADS], hl[:, :, 0, :], cl, hs, cs)

    x_meta = meta_tokens[None]
    x_main = x_prompt
    x_samp = x_sample
    st_meta = st_main = st_samp = None
    for l in range(DEPTH):
        last = l == DEPTH - 1
        x_meta, st_meta = run_short(l, x_meta, zero_states, 0, st_meta, N_META, 1, need_x=not last)
        x_main, st_main = _layer(x_main, w_tiles[l], w_small[l], norm_g[l][None], st_meta, l, st_main, l,
                                 layer_params(l), (wml[l], wlru[l], wssd[l], wo[l], fg), last, 128, 2)
        x_samp, st_samp = run_short(l, x_samp, sample_states, l, st_samp, dseq, 8, need_x=True)

    return (x_main, x_samp) + unpack(st_main) + unpack(st_samp)
```
